```python
import jax, jax.numpy as jnp
from jax import lax
import numpy as np

D_MODEL = 1024
BATCH = 16
SEQ = 4096
DEPTH = 1
DEC_BATCH = 128
DEC_SEQ = 4
PAST_LEN = 8192
PAGE_SIZE = 128

D_MIX = D_MODEL
HEAD_DIM = 64
D_CONV = D_MIX // 2
D_ATT = D_MIX - D_CONV
N_CONV_GROUPS = D_CONV // HEAD_DIM
N_HEADS = D_ATT // HEAD_DIM
N_MIX_HEADS = N_CONV_GROUPS + N_HEADS
CONV_WIDTH = 3
D_FF = -(-8 * D_MODEL // (3 * 256)) * 256
Q_BLOCK = 128
FORGET_BIAS_INIT = 3.0
EPS = 1e-6
D_IN = 3 * D_CONV + 3 * D_ATT + N_HEADS
SPLITS = (D_CONV, 2 * D_CONV, 3 * D_CONV, 3 * D_CONV + D_ATT, 3 * D_CONV + 2 * D_ATT, 3 * D_CONV + 3 * D_ATT)

kernel_name = "hymba_conv_fox_decode_step"


def rmsnorm(x, g):
    xf = x.astype(jnp.float32)
    y = xf * lax.rsqrt(jnp.mean(xf * xf, axis=-1, keepdims=True) + EPS)
    return (y * g.astype(jnp.float32)).astype(x.dtype)


def mixer_inputs(x, norm_g, w_in, b_f, q_g, k_g):
    xn = rmsnorm(x, norm_g)
    z = xn @ w_in
    ub, uc, ux, q, k, v, fl = jnp.split(z, SPLITS, axis=-1)
    lead = x.shape[:-1]
    q = rmsnorm(q.reshape(lead + (N_HEADS, HEAD_DIM)), q_g)
    k = rmsnorm(k.reshape(lead + (N_HEADS, HEAD_DIM)), k_g)
    v = v.reshape(lead + (N_HEADS, HEAD_DIM))
    logf = jax.nn.log_sigmoid((fl + b_f).astype(jnp.float32))
    u = uc * ux
    return ub, u, q, k, v, logf


def causal_conv(u_cat, w, t):
    return sum(w[j] * u_cat[:, j:j + t] for j in range(CONV_WIDTH))


def mix_out(y_conv, y_att, out_g, w_out):
    lead = y_conv.shape[:-1]
    y = jnp.concatenate([y_conv, y_att.reshape(lead + (D_ATT,))], axis=-1)
    y = rmsnorm(y.reshape(lead + (N_MIX_HEADS, HEAD_DIM)), out_g.reshape(N_MIX_HEADS, HEAD_DIM))
    return y.reshape(lead + (D_MIX,)) @ w_out


def swiglu(h, g, w_gate, w_up, w_down):
    hn = rmsnorm(h, g)
    return (jax.nn.silu(hn @ w_gate) * (hn @ w_up)) @ w_down


def fox_scores(q, k_all, cq, c_all):
    s = jnp.einsum('bqhd,bkhd->bhqk', q, k_all).astype(jnp.float32) * (HEAD_DIM ** -0.5)
    return s + jnp.transpose(cq, (0, 2, 1))[:, :, :, None] - jnp.transpose(c_all, (0, 2, 1))[:, :, None, :]


def prompt_attention(q, k, v, logf):
    b, s_len = q.shape[0], q.shape[1]
    c = jnp.cumsum(logf, axis=1)
    key_pos = jnp.arange(s_len)

    def block(i):
        start = i * Q_BLOCK
        qb = lax.dynamic_slice_in_dim(q, start, Q_BLOCK, axis=1)
        cq = lax.dynamic_slice_in_dim(c, start, Q_BLOCK, axis=1)
        s = fox_scores(qb, k, cq, c)
        mask = key_pos[None, :] <= (start + jnp.arange(Q_BLOCK))[:, None]
        p = jax.nn.softmax(jnp.where(mask, s, -jnp.inf), axis=-1)
        return jnp.einsum('bhqk,bkhd->bqhd', p.astype(v.dtype), v)

    o = lax.map(block, jnp.arange(s_len // Q_BLOCK))
    return jnp.moveaxis(o, 0, 1).reshape(b, s_len, N_HEADS, HEAD_DIM)


def sample_attention(q, k, v, logf, k_past, v_past, logf_past):
    p_len, t = k_past.shape[1], q.shape[1]
    k_all = jnp.concatenate([k_past, k], axis=1)
    v_all = jnp.concatenate([v_past, v], axis=1)
    c = jnp.cumsum(jnp.concatenate([logf_past.astype(jnp.float32), logf], axis=1), axis=1)
    s = fox_scores(q, k_all, c[:, p_len:], c)
    mask = jnp.arange(p_len + t)[None, :] <= (p_len + jnp.arange(t))[:, None]
    p = jax.nn.softmax(jnp.where(mask, s, -jnp.inf), axis=-1)
    return jnp.einsum('bhqk,bkhd->bqhd', p.astype(v_all.dtype), v_all)


def setup_inputs(seed: int = 0) -> dict:
    key = jax.random.key(seed)
    ks = jax.random.split(key, 24)
    n_pages = PAST_LEN // PAGE_SIZE
    n_used = DEC_BATCH * n_pages
    n_phys = n_used + (n_used + 3) // 4
    f32 = jnp.float32
    nrm = lambda k, shape, s=1.0: (jax.random.normal(k, shape, f32) * s)
    gain = lambda k, shape: 1.0 + 0.02 * jax.random.normal(k, shape, f32)
    page_table = jax.random.permutation(ks[0], n_phys)[:n_used].reshape(DEC_BATCH, n_pages).astype(jnp.int32)
    return {
        "x_prompt": nrm(ks[1], (BATCH, SEQ, D_MODEL)),
        "x_sample": nrm(ks[2], (DEC_BATCH, DEC_SEQ, D_MODEL)),
        "cache_k": nrm(ks[3], (DEPTH, n_phys, PAGE_SIZE, N_HEADS, HEAD_DIM)),
        "cache_v": nrm(ks[4], (DEPTH, n_phys, PAGE_SIZE, N_HEADS, HEAD_DIM)),
        "cache_logf": jax.nn.log_sigmoid(FORGET_BIAS_INIT + nrm(ks[5], (DEPTH, n_phys, PAGE_SIZE, N_HEADS))),
        "state_conv": nrm(ks[6], (DEPTH, DEC_BATCH, CONV_WIDTH - 1, D_CONV)),
        "page_table": page_table,
        "norm_mix_g": gain(ks[7], (DEPTH, D_MODEL)),
        "w_in": nrm(ks[8], (DEPTH, D_MODEL, D_IN), D_MODEL ** -0.5),
        "b_f": FORGET_BIAS_INIT + nrm(ks[9], (DEPTH, N_HEADS), 0.1),
        "conv_w": nrm(ks[10], (DEPTH, CONV_WIDTH, D_CONV), CONV_WIDTH ** -0.5),
        "q_norm_g": gain(ks[11], (DEPTH, HEAD_DIM)),
        "k_norm_g": gain(ks[12], (DEPTH, HEAD_DIM)),
        "out_norm_g": gain(ks[13], (DEPTH, D_MIX)),
        "w_out": nrm(ks[14], (DEPTH, D_MIX, D_MODEL), D_MIX ** -0.5),
        "norm_ffn_g": gain(ks[15], (DEPTH, D_MODEL)),
        "w_gate": nrm(ks[16], (DEPTH, D_MODEL, D_FF), D_MODEL ** -0.5),
        "w_up": nrm(ks[17], (DEPTH, D_MODEL, D_FF), D_MODEL ** -0.5),
        "w_down": nrm(ks[18], (DEPTH, D_FF, D_MODEL), D_FF ** -0.5),
    }


def reference(x_prompt, x_sample, cache_k, cache_v, cache_logf, state_conv, page_table,
              norm_mix_g, w_in, b_f, conv_w, q_norm_g, k_norm_g, out_norm_g, w_out,
              norm_ffn_g, w_gate, w_up, w_down):
    dec_b, n_pages = page_table.shape
    past = n_pages * PAGE_SIZE
    xp, xs = x_prompt, x_sample
    nkp, nvp, nlp, ncp, nks, nvs, nls, ncs = ([] for _ in range(8))
    for l in range(DEPTH):
        ub, u, q, k, v, logf = mixer_inputs(xp, norm_mix_g[l], w_in[l], b_f[l], q_norm_g[l], k_norm_g[l])
        u_cat = jnp.concatenate([jnp.zeros((u.shape[0], CONV_WIDTH - 1, D_CONV), u.dtype), u], axis=1)
        y_conv = ub * causal_conv(u_cat, conv_w[l], u.shape[1])
        y_att = prompt_attention(q, k, v, logf)
        h = xp + mix_out(y_conv, y_att, out_norm_g[l], w_out[l])
        xp = h + swiglu(h, norm_ffn_g[l], w_gate[l], w_up[l], w_down[l])
        nkp.append(k); nvp.append(v); nlp.append(logf.astype(xp.dtype)); ncp.append(u_cat[:, -(CONV_WIDTH - 1):])
        ub, u, q, k, v, logf = mixer_inputs(xs, norm_mix_g[l], w_in[l], b_f[l], q_norm_g[l], k_norm_g[l])
        u_cat = jnp.concatenate([state_conv[l].astype(u.dtype), u], axis=1)
        y_conv = ub * causal_conv(u_cat, conv_w[l], u.shape[1])
        k_past = cache_k[l][page_table].reshape(dec_b, past, N_HEADS, HEAD_DIM)
        v_past = cache_v[l][page_table].reshape(dec_b, past, N_HEADS, HEAD_DIM)
        logf_past = cache_logf[l][page_table].reshape(dec_b, past, N_HEADS)
        y_att = sample_attention(q, k, v, logf, k_past, v_past, logf_past)
        h = xs + mix_out(y_conv, y_att, out_norm_g[l], w_out[l])
        xs = h + swiglu(h, norm_ffn_g[l], w_gate[l], w_up[l], w_down[l])
        nks.append(k); nvs.append(v); nls.append(logf.astype(xs.dtype)); ncs.append(u_cat[:, -(CONV_WIDTH - 1):])
    return (xp, xs, jnp.stack(nkp), jnp.stack(nvp), jnp.stack(nlp), jnp.stack(ncp),
            jnp.stack(nks), jnp.stack(nvs), jnp.stack(nls), jnp.stack(ncs))
```

```python
import functools

import jax
import jax.numpy as jnp
from jax import lax
from jax.experimental import pallas as pl
from jax.experimental.pallas import tpu as pltpu

HEAD_DIM = 64
CONV_WIDTH = 3
PAGE_SIZE = 128
EPS = 1e-6
LANES = 128
MXU_DIM = 256
VMEM_LIMIT = 56 * 1024 * 1024

F32 = jnp.float32
BF16 = jnp.bfloat16


def _log_sigmoid(x):
    return jnp.minimum(x, 0.0) - jnp.log1p(jnp.exp(-jnp.abs(x)))


def _rms_scale(x):
    return lax.rsqrt(jnp.mean(x * x, axis=-1, keepdims=True) + EPS)


def _head_mean_sq(y, seg):
    y2 = (y * y).astype(BF16)
    parts = [jnp.dot(y2[:, c:c + MXU_DIM], seg, preferred_element_type=F32)
             for c in range(0, y.shape[1], MXU_DIM)]
    return parts[0] if len(parts) == 1 else jnp.concatenate(parts, axis=1)


def _head_norm(y, gain, seg):
    return y * lax.rsqrt(_head_mean_sq(y, seg) + EPS) * gain


def _dotT(a, b):
    return lax.dot_general(a, b, (((1,), (1,)), ((), ())), preferred_element_type=F32)


def _const_spec(shape):
    nd = len(shape)
    return pl.BlockSpec(shape, lambda *_: (0,) * nd, pipeline_mode=pl.Buffered(1))


def _inproj_prompt_kernel(x_ref, g_ref, w_ref, wft_ref, bf_ref, cw_ref, qg_ref, kg_ref, og_ref, seg_ref,
                          ycn_ref, qb_ref, kb_ref, vb_ref, kout_ref, vout_ref, lft_ref, ct_ref, cs_ref,
                          ucar_ref, ccar_ref, *, dc, da):
    j = pl.program_id(1)
    tm = x_ref.shape[1]

    @pl.when(j == 0)
    def _():
        ucar_ref[...] = jnp.zeros_like(ucar_ref)
        ccar_ref[...] = jnp.zeros_like(ccar_ref)

    x = x_ref[0]
    xn = (x * _rms_scale(x) * g_ref[...]).astype(BF16)
    seg = seg_ref[...]

    def proj(c0, n):
        return jnp.dot(xn, w_ref[:, c0:c0 + n], preferred_element_type=F32)

    ub = proj(0, dc)
    u = proj(dc, dc) * proj(2 * dc, dc)
    prev = ucar_ref[...]
    row = lax.broadcasted_iota(jnp.int32, (tm, 1), 0)
    u1 = jnp.where(row == 0, prev[7:8], pltpu.roll(u, 1, 0))
    u2 = jnp.where(row == 0, prev[6:7], jnp.where(row == 1, prev[7:8], pltpu.roll(u, 2, 0)))
    yc = ub * (cw_ref[0:1] * u2 + cw_ref[1:2] * u1 + cw_ref[2:3] * u)
    ycn_ref[0] = _head_norm(yc, og_ref[...], seg).astype(BF16)
    ucar_ref[...] = u[tm - 8:tm]
    cs_ref[0] = u[tm - (CONV_WIDTH - 1):tm]

    q = _head_norm(proj(3 * dc, da), qg_ref[...], seg)
    qb_ref[0] = (q * (HEAD_DIM ** -0.5)).astype(BF16)
    k = _head_norm(proj(3 * dc + da, da), kg_ref[...], seg)
    kout_ref[0] = k
    kb_ref[0] = k.astype(BF16)
    v = proj(3 * dc + 2 * da, da)
    vout_ref[0] = v
    vb_ref[0] = v.astype(BF16)

    nh = lft_ref.shape[1]
    lf = _log_sigmoid(_dotT(wft_ref[...], xn)[:nh] + bf_ref[...])
    lft_ref[0] = lf
    lane = lax.broadcasted_iota(jnp.int32, lf.shape, 1)
    c = lf
    sh = 1
    while sh < tm:
        c = c + jnp.where(lane >= sh, pltpu.roll(c, sh, 1), 0.0)
        sh *= 2
    c = c + ccar_ref[:, 0:1]
    ct_ref[0] = c
    ccar_ref[...] = jnp.broadcast_to(c[:, tm - 1:tm], ccar_ref.shape)


def _inproj_prompt(x, g, w, wft, bfc, cw, qg, kg, og, seg, *, tm):
    b, s, d = x.shape
    dc = cw.shape[1]
    da = qg.shape[1]
    nh = bfc.shape[0]
    grid = (b, s // tm)
    row_blk = lambda n: pl.BlockSpec((1, tm, n), lambda i, j: (i, j, 0))
    out_shape = (
        jax.ShapeDtypeStruct((b, s, dc), BF16),
        jax.ShapeDtypeStruct((b, s, da), BF16),
        jax.ShapeDtypeStruct((b, s, da), BF16),
        jax.ShapeDtypeStruct((b, s, da), BF16),
        jax.ShapeDtypeStruct((b, s, da), F32),
        jax.ShapeDtypeStruct((b, s, da), F32),
        jax.ShapeDtypeStruct((b, nh, s), F32),
        jax.ShapeDtypeStruct((b, nh, s), F32),
        jax.ShapeDtypeStruct((b, CONV_WIDTH - 1, dc), F32),
    )
    out_specs = (
        row_blk(dc), row_blk(da), row_blk(da), row_blk(da), row_blk(da), row_blk(da),
        pl.BlockSpec((1, nh, tm), lambda i, j: (i, 0, j)),
        pl.BlockSpec((1, nh, tm), lambda i, j: (i, 0, j)),
        pl.BlockSpec((1, CONV_WIDTH - 1, dc), lambda i, j: (i, 0, 0)),
    )
    in_specs = [
        row_blk(d), _const_spec(g.shape), _const_spec(w.shape), _const_spec(wft.shape),
        _const_spec(bfc.shape), _const_spec(cw.shape), _const_spec(qg.shape), _const_spec(kg.shape),
        _const_spec(og.shape), _const_spec(seg.shape),
    ]
    return pl.pallas_call(
        functools.partial(_inproj_prompt_kernel, dc=dc, da=da),
        grid=grid, in_specs=in_specs, out_specs=out_specs, out_shape=out_shape,
        scratch_shapes=[pltpu.VMEM((8, dc), F32), pltpu.VMEM((nh, LANES), F32)],
        compiler_params=pltpu.CompilerParams(
            dimension_semantics=("arbitrary", "arbitrary"), vmem_limit_bytes=VMEM_LIMIT),
        name="inproj_prompt",
    )(x, g, w, wft, bfc, cw, qg, kg, og, seg)


def _attn_prompt_kernel(q_ref, k_ref, v_ref, c_ref, og_ref, seg_ref, o_ref, m_ref, l_ref, acc_ref, *, tk):
    hp = pl.program_id(1)
    qi = pl.program_id(2)
    tq = q_ref.shape[1]
    q2 = q_ref[0]
    lane = lax.broadcasted_iota(jnp.int32, (1, LANES), 1)
    low = lane < HEAD_DIM
    row = lax.broadcasted_iota(jnp.int32, (tq, tk), 0)
    col = lax.broadcasted_iota(jnp.int32, (tq, tk), 1)
    outs = []
    for hh in range(2):
        qh = jnp.where(low if hh == 0 else jnp.logical_not(low), q2, jnp.zeros_like(q2))
        h = 2 * hp + hh
        m_ref[...] = jnp.full_like(m_ref, -jnp.inf)
        l_ref[...] = jnp.zeros_like(l_ref)
        acc_ref[...] = jnp.zeros_like(acc_ref)

        def tile(ki, masked):
            k0 = pl.multiple_of(ki * tk, tk)
            s = _dotT(qh, k_ref[0, pl.ds(k0, tk), :]) - c_ref[0, h, pl.ds(ki, 1), :]
            if masked:
                s = jnp.where(col <= row, s, -jnp.inf)
            m_old = m_ref[...]
            m_new = jnp.maximum(m_old, jnp.max(s, axis=1, keepdims=True))
            alpha = jnp.exp(m_old - m_new)
            p = jnp.exp(s - m_new)
            l_ref[...] = alpha * l_ref[...] + jnp.sum(p, axis=1, keepdims=True)
            acc_ref[...] = alpha * acc_ref[...] + jnp.dot(
                p.astype(BF16), v_ref[0, pl.ds(k0, tk), :], preferred_element_type=F32)
            m_ref[...] = m_new

        def body(ki, carry):
            tile(ki, False)
            return carry

        lax.fori_loop(0, qi, body, 0)
        tile(qi, True)
        outs.append(acc_ref[...] / l_ref[...])
    o = jnp.where(low, outs[0], outs[1])
    ms = jnp.dot((o * o).astype(BF16), seg_ref[...], preferred_element_type=F32)
    o_ref[0] = (o * lax.rsqrt(ms + EPS) * og_ref[...]).astype(BF16)


def _attn_prompt(qb, kb, vb, ct4, og, seg, *, tq):
    b, s, da = qb.shape
    nh, nk, tk = ct4.shape[1:]
    assert tk == tq
    grid = (b, da // LANES, s // tq)
    in_specs = [
        pl.BlockSpec((1, tq, LANES), lambda i, h, j: (i, j, h)),
        pl.BlockSpec((1, s, LANES), lambda i, h, j: (i, 0, h)),
        pl.BlockSpec((1, s, LANES), lambda i, h, j: (i, 0, h)),
        pl.BlockSpec((1, nh, nk, tk), lambda i, h, j: (i, 0, 0, 0)),
        pl.BlockSpec((1, LANES), lambda i, h, j: (0, h)),
        _const_spec(seg.shape),
    ]
    return pl.pallas_call(
        functools.partial(_attn_prompt_kernel, tk=tk),
        grid=grid, in_specs=in_specs,
        out_specs=pl.BlockSpec((1, tq, LANES), lambda i, h, j: (i, j, h)),
        out_shape=jax.ShapeDtypeStruct((b, s, da), BF16),
        scratch_shapes=[pltpu.VMEM((tq, 1), F32), pltpu.VMEM((tq, 1), F32), pltpu.VMEM((tq, LANES), F32)],
        compiler_params=pltpu.CompilerParams(
            dimension_semantics=("arbitrary", "arbitrary", "arbitrary"), vmem_limit_bytes=VMEM_LIMIT),
        name="attn_prompt",
    )(qb, kb, vb, ct4, og, seg)


def _mix_ffn_kernel(x_ref, yc_ref, ya_ref, wo_ref, fg_ref, wg_ref, wu_ref, wd_ref, o_ref, *, ff_chunk):
    dc = yc_ref.shape[1]
    mix = jnp.dot(yc_ref[...], wo_ref[0:dc, :], preferred_element_type=F32)
    mix = mix + jnp.dot(ya_ref[...], wo_ref[dc:, :], preferred_element_type=F32)
    h = x_ref[...] + mix
    hn = (h * _rms_scale(h) * fg_ref[...]).astype(BF16)
    dff = wg_ref.shape[1]
    acc = None
    for c0 in range(0, dff, ff_chunk):
        gate = jnp.dot(hn, wg_ref[:, c0:c0 + ff_chunk], preferred_element_type=F32)
        up = jnp.dot(hn, wu_ref[:, c0:c0 + ff_chunk], preferred_element_type=F32)
        a = (gate * jax.nn.sigmoid(gate) * up).astype(BF16)
        part = jnp.dot(a, wd_ref[c0:c0 + ff_chunk, :], preferred_element_type=F32)
        acc = part if acc is None else acc + part
    o_ref[...] = h + acc


def _mix_ffn(x, yc, ya, wo, fg, wg, wu, wd, *, tm, ff_chunk):
    r, d = x.shape
    row_blk = lambda n: pl.BlockSpec((tm, n), lambda i: (i, 0))
    in_specs = [row_blk(d), row_blk(yc.shape[1]), row_blk(ya.shape[1]),
                _const_spec(wo.shape), _const_spec(fg.shape), _const_spec(wg.shape),
                _const_spec(wu.shape), _const_spec(wd.shape)]
    return pl.pallas_call(
        functools.partial(_mix_ffn_kernel, ff_chunk=ff_chunk),
        grid=(r // tm,), in_specs=in_specs, out_specs=row_blk(d),
        out_shape=jax.ShapeDtypeStruct((r, d), F32),
        compiler_params=pltpu.CompilerParams(
            dimension_semantics=("arbitrary",), vmem_limit_bytes=VMEM_LIMIT),
        name="mix_ffn",
    )(x, yc, ya, wo, fg, wg, wu, wd)


def _inproj_sample_kernel(x_ref, g_ref, w_ref, wft_ref, bf_ref, cw_ref, qg_ref, kg_ref, og_ref, seg_ref,
                          st_ref, ycn_ref, q_ref, k_ref, v_ref, lft_ref, gt_ref, cs_ref, *, dc, da, nt):
    nb = x_ref.shape[0] // nt
    x = x_ref[...]
    xn = (x * _rms_scale(x) * g_ref[...]).astype(BF16)
    seg = seg_ref[...]

    def proj(c0, n):
        return jnp.dot(xn, w_ref[:, c0:c0 + n], preferred_element_type=F32)

    ub = proj(0, dc)
    u = proj(dc, dc) * proj(2 * dc, dc)
    ucat = [st_ref[i] for i in range(CONV_WIDTH - 1)] + [u[t * nb:(t + 1) * nb] for t in range(nt)]
    conv = jnp.concatenate(
        [sum(cw_ref[jj:jj + 1] * ucat[t + jj] for jj in range(CONV_WIDTH)) for t in range(nt)], axis=0)
    ycn_ref[...] = _head_norm(ub * conv, og_ref[...], seg).astype(BF16)
    for i in range(CONV_WIDTH - 1):
        cs_ref[i] = ucat[nt + i]

    q_ref[...] = _head_norm(proj(3 * dc, da), qg_ref[...], seg) * (HEAD_DIM ** -0.5)
    k_ref[...] = _head_norm(proj(3 * dc + da, da), kg_ref[...], seg)
    v_ref[...] = proj(3 * dc + 2 * da, da)

    nh = lft_ref.shape[0]
    lf = _log_sigmoid(_dotT(wft_ref[...], xn)[:nh] + bf_ref[...])
    lft_ref[...] = lf
    run = lf[:, 0:nb]
    gs = [run]
    for t in range(1, nt):
        run = run + lf[:, t * nb:(t + 1) * nb]
        gs.append(run)
    gt_ref[...] = jnp.concatenate(gs, axis=1)


def _inproj_sample(x, g, w, wft, bfc, cw, qg, kg, og, seg, st, *, nt):
    r, d = x.shape
    dc = cw.shape[1]
    da = qg.shape[1]
    nh = bfc.shape[0]
    args = (x, g, w, wft, bfc, cw, qg, kg, og, seg, st)
    out_shape = (
        jax.ShapeDtypeStruct((r, dc), BF16),
        jax.ShapeDtypeStruct((r, da), F32),
        jax.ShapeDtypeStruct((r, da), F32),
        jax.ShapeDtypeStruct((r, da), F32),
        jax.ShapeDtypeStruct((nh, r), F32),
        jax.ShapeDtypeStruct((nh, r), F32),
        jax.ShapeDtypeStruct(st.shape, F32),
    )
    full = lambda a: pl.BlockSpec(a.shape, lambda i: (0,) * len(a.shape))
    return pl.pallas_call(
        functools.partial(_inproj_sample_kernel, dc=dc, da=da, nt=nt),
        grid=(1,), in_specs=[full(a) for a in args], out_specs=tuple(full(o) for o in out_shape),
        out_shape=out_shape,
        compiler_params=pltpu.CompilerParams(
            dimension_semantics=("arbitrary",), vmem_limit_bytes=VMEM_LIMIT),
        name="inproj_sample",
    )(*args)


def _attn_decode_kernel(pt_ref, qbd_ref, kn_ref, vn_ref, gneg_ref, og_ref, seg_ref, *rest, pps, nt, nh):
    k_refs = rest[0:pps]
    v_refs = rest[pps:2 * pps]
    lf_refs = rest[2 * pps:3 * pps]
    o_ref, m_ref, l_ref, acc_ref, run_ref = rest[3 * pps:]
    step = pl.program_id(1)
    qbd = qbd_ref[0]
    nr = qbd.shape[0]

    @pl.when(step == 0)
    def _():
        sn = _dotT(qbd, kn_ref[0]) + gneg_ref[0]
        r_i = lax.broadcasted_iota(jnp.int32, sn.shape, 0)
        c_i = lax.broadcasted_iota(jnp.int32, sn.shape, 1)
        sn = jnp.where(c_i * nh <= r_i, sn, -jnp.inf)
        m0 = jnp.max(sn, axis=1, keepdims=True)
        p0 = jnp.exp(sn - m0)
        m_ref[...] = m0
        l_ref[...] = jnp.sum(p0, axis=1, keepdims=True)
        acc_ref[...] = jnp.dot(p0.astype(BF16), vn_ref[0], preferred_element_type=F32)
        run_ref[...] = jnp.zeros_like(run_ref)

    lane = lax.broadcasted_iota(jnp.int32, (nh, PAGE_SIZE), 1)
    run = run_ref[...]
    scores = []
    for i in range(pps):
        lf = lf_refs[i][0]
        suf = lf
        sh = 1
        while sh < PAGE_SIZE:
            suf = suf + jnp.where(lane < PAGE_SIZE - sh, pltpu.roll(suf, PAGE_SIZE - sh, 1), 0.0)
            sh *= 2
        decay = run + (suf - lf)
        run = run + suf[:, 0:1]
        scores.append(_dotT(qbd, k_refs[i][0].astype(BF16)) + jnp.concatenate([decay] * nt, axis=0))
    run_ref[...] = run
    s = jnp.concatenate(scores, axis=1)
    m_old = m_ref[...]
    m_new = jnp.maximum(m_old, jnp.max(s, axis=1, keepdims=True))
    alpha = jnp.exp(m_old - m_new)
    p = jnp.exp(s - m_new)
    l_ref[...] = alpha * l_ref[...] + jnp.sum(p, axis=1, keepdims=True)
    pv = None
    for i in range(pps):
        part = jnp.dot(p[:, i * PAGE_SIZE:(i + 1) * PAGE_SIZE].astype(BF16), v_refs[i][0].astype(BF16),
                       preferred_element_type=F32)
        pv = part if pv is None else pv + part
    acc_ref[...] = alpha * acc_ref[...] + pv
    m_ref[...] = m_new

    @pl.when(step == pl.num_programs(1) - 1)
    def _():
        o = acc_ref[...] / l_ref[...]
        r_i = lax.broadcasted_iota(jnp.int32, o.shape, 0)
        c_i = lax.broadcasted_iota(jnp.int32, o.shape, 1)
        o = jnp.where(c_i // HEAD_DIM == r_i % nh, o, 0.0)
        y = jnp.concatenate([jnp.sum(o[t * nh:(t + 1) * nh], axis=0, keepdims=True) for t in range(nt)], axis=0)
        o_ref[0] = _head_norm(y, og_ref[...], seg_ref[...])


def _attn_decode(page_table, qbd, kn, vn, gneg, og, seg, ck, cv, clf, *, pps, nt, nh):
    nb, npages = page_table.shape
    nr, da = qbd.shape[1:]
    steps = npages // pps

    def page_map(i):
        return lambda b, s, pt: (pt[b, npages - 1 - (s * pps + i)], 0, 0)

    per_b = lambda a: pl.BlockSpec((1,) + a.shape[1:], lambda b, s, pt: (b,) + (0,) * (a.ndim - 1))
    const = lambda a: pl.BlockSpec(a.shape, lambda b, s, pt: (0,) * a.ndim)
    in_specs = [per_b(qbd), per_b(kn), per_b(vn), per_b(gneg), const(og), const(seg)]
    in_specs += [pl.BlockSpec((1, PAGE_SIZE, da), page_map(i)) for i in range(pps)]
    in_specs += [pl.BlockSpec((1, PAGE_SIZE, da), page_map(i)) for i in range(pps)]
    in_specs += [pl.BlockSpec((1, nh, PAGE_SIZE), page_map(i)) for i in range(pps)]
    grid_spec = pltpu.PrefetchScalarGridSpec(
        num_scalar_prefetch=1, grid=(nb, steps), in_specs=in_specs,
        out_specs=pl.BlockSpec((1, nt, da), lambda b, s, pt: (b, 0, 0)),
        scratch_shapes=[pltpu.VMEM((nr, 1), F32), pltpu.VMEM((nr, 1), F32),
                        pltpu.VMEM((nr, da), F32), pltpu.VMEM((nh, 1), F32)])
    return pl.pallas_call(
        functools.partial(_attn_decode_kernel, pps=pps, nt=nt, nh=nh),
        grid_spec=grid_spec,
        out_shape=jax.ShapeDtypeStruct((nb, nt, da), F32),
        compiler_params=pltpu.CompilerParams(
            dimension_semantics=("arbitrary", "arbitrary"), vmem_limit_bytes=VMEM_LIMIT),
        name="attn_decode",
    )(page_table, qbd, kn, vn, gneg, og, seg, *([ck] * pps), *([cv] * pps), *([clf] * pps))


def _pick(n, pref):
    t = min(n, pref)
    while n % t:
        t //= 2
    return t


def _layer(xp, xs, cache_k, cache_v, cache_logf, state_conv, page_table,
           norm_g, w_in, b_f, conv_w, q_g, k_g, out_g, w_out, ffn_g, w_gate, w_up, w_down):
    bsz, seq, d = xp.shape
    nb, nt, _ = xs.shape
    dc = conv_w.shape[1]
    nh = b_f.shape[0]
    da = nh * HEAD_DIM
    npages = page_table.shape[1]

    g = norm_g.reshape(1, d)
    w_main = w_in[:, :3 * dc + 3 * da].astype(BF16)
    wft = jnp.zeros((16, d), BF16).at[:nh].set(w_in[:, 3 * dc + 3 * da:].T.astype(BF16))
    bfc = b_f.reshape(nh, 1)
    qg = jnp.tile(q_g, nh).reshape(1, da)
    kg = jnp.tile(k_g, nh).reshape(1, da)
    og_c = out_g[:dc].reshape(1, dc)
    og_a = out_g[dc:].reshape(1, da)
    gi = jnp.arange(MXU_DIM) // HEAD_DIM
    seg = jnp.where(gi[:, None] == gi[None, :], 1.0 / HEAD_DIM, 0.0).astype(BF16)
    wo = w_out.astype(BF16)
    fg = ffn_g.reshape(1, d)
    wg = w_gate.astype(BF16)
    wu = w_up.astype(BF16)
    wd = w_down.astype(BF16)
    dff = wg.shape[1]

    tm = _pick(seq, 512)
    ycn, qb, kb, vb, k_out, v_out, lft, ct, cs_p = _inproj_prompt(
        xp, g, w_main, wft, bfc, conv_w, qg, kg, og_c, seg, tm=tm)
    tq = _pick(seq, 512)
    yan = _attn_prompt(qb, kb, vb, ct.reshape(bsz, nh, seq // tq, tq), og_a, seg[:LANES, :LANES], tq=tq)
    rows = bsz * seq
    yp = _mix_ffn(xp.reshape(rows, d), ycn.reshape(rows, dc), yan.reshape(rows, da),
                  wo, fg, wg, wu, wd, tm=_pick(rows, 512), ff_chunk=dff)
    y_prompt = yp.reshape(bsz, seq, d)
    nk_p = k_out.reshape(bsz, seq, nh, HEAD_DIM)
    nv_p = v_out.reshape(bsz, seq, nh, HEAD_DIM)
    nl_p = jnp.transpose(lft, (0, 2, 1))

    rs = nb * nt
    xs_tm = jnp.transpose(xs, (1, 0, 2)).reshape(rs, d)
    st = jnp.transpose(state_conv, (1, 0, 2))
    ycn_s, q_s, k_s, v_s, lft_s, gt_s, cs_s = _inproj_sample(
        xs_tm, g, w_main, wft, bfc, conv_w, qg, kg, og_c, seg, st, nt=nt)

    to_bt = lambda a: jnp.transpose(a.reshape(nt, nb, -1), (1, 0, 2))
    eye = jnp.eye(nh, dtype=F32)
    qbd = jnp.einsum('tbhd,hg->bthgd', q_s.reshape(nt, nb, nh, HEAD_DIM), eye)
    qbd = qbd.reshape(nb, nt * nh, da).astype(BF16)
    pad_rows = 16 - nt
    kn = jnp.pad(to_bt(k_s), ((0, 0), (0, pad_rows), (0, 0))).astype(BF16)
    vn = jnp.pad(to_bt(v_s), ((0, 0), (0, pad_rows), (0, 0))).astype(BF16)
    gneg = -jnp.transpose(gt_s.reshape(nh, nt, nb), (2, 0, 1))
    gneg = jnp.pad(gneg, ((0, 0), (0, 0), (0, 16 - nt)))
    gneg = jnp.tile(gneg, (1, nt, 1))
    nphys = cache_k.shape[0]
    ck = cache_k.reshape(nphys, PAGE_SIZE, da)
    cv = cache_v.reshape(nphys, PAGE_SIZE, da)
    clf = jnp.transpose(cache_logf, (0, 2, 1))
    pps = _pick(npages, 8)
    yan_s = _attn_decode(page_table, qbd, kn, vn, gneg, og_a, seg, ck, cv, clf, pps=pps, nt=nt, nh=nh)
    ys = _mix_ffn(xs.reshape(rs, d), to_bt(ycn_s).reshape(rs, dc), yan_s.reshape(rs, da).astype(BF16),
                  wo, fg, wg, wu, wd, tm=_pick(rs, 512), ff_chunk=dff)
    y_sample = ys.reshape(nb, nt, d)
    nk_s = to_bt(k_s).reshape(nb, nt, nh, HEAD_DIM)
    nv_s = to_bt(v_s).reshape(nb, nt, nh, HEAD_DIM)
    nl_s = jnp.transpose(lft_s.reshape(nh, nt, nb), (2, 1, 0))
    cs_s = jnp.transpose(cs_s, (1, 0, 2))
    return (y_prompt, y_sample, nk_p, nv_p, nl_p, cs_p, nk_s, nv_s, nl_s, cs_s)


def kernel(x_prompt, x_sample, cache_k, cache_v, cache_logf, state_conv, page_table,
           norm_mix_g, w_in, b_f, conv_w, q_norm_g, k_norm_g, out_norm_g, w_out,
           norm_ffn_g, w_gate, w_up, w_down):
    depth = w_in.shape[0]
    xp, xs = x_prompt, x_sample
    outs = []
    for l in range(depth):
        res = _layer(xp, xs, cache_k[l], cache_v[l], cache_logf[l], state_conv[l], page_table,
                     norm_mix_g[l], w_in[l], b_f[l], conv_w[l], q_norm_g[l], k_norm_g[l], out_norm_g[l],
                     w_out[l], norm_ffn_g[l], w_gate[l], w_up[l], w_down[l])
        xp, xs = res[0], res[1]
        outs.append(res[2:])
    stacked = tuple(jnp.stack([o[i] for o in outs]) for i in range(8))
    return (xp, xs) + stacked
```

```python
import functools

import jax
import jax.numpy as jnp
from jax import lax
from jax.experimental import pallas as pl
from jax.experimental.pallas import tpu as pltpu

HEAD_DIM = 64
CONV_WIDTH = 3
PAGE_SIZE = 128
EPS = 1e-6
LOG2E = 1.4426950408889634
LANES = 128
MXU_DIM = 256
VMEM_LIMIT = 56 * 1024 * 1024

F32 = jnp.float32
BF16 = jnp.bfloat16


def _log_sigmoid(x):
    return jnp.minimum(x, 0.0) - jnp.log1p(jnp.exp(-jnp.abs(x)))


def _rms_scale(x):
    return lax.rsqrt(jnp.mean(x * x, axis=-1, keepdims=True) + EPS)


def _head_mean_sq(y, seg):
    y2 = (y * y).astype(BF16)
    parts = [jnp.dot(y2[:, c:c + MXU_DIM], seg, preferred_element_type=F32)
             for c in range(0, y.shape[1], MXU_DIM)]
    return parts[0] if len(parts) == 1 else jnp.concatenate(parts, axis=1)


def _head_norm(y, gain, seg):
    return y * lax.rsqrt(_head_mean_sq(y, seg) + EPS) * gain


def _dotT(a, b):
    return lax.dot_general(a, b, (((1,), (1,)), ((), ())), preferred_element_type=F32)


def _const_spec(shape):
    nd = len(shape)
    return pl.BlockSpec(shape, lambda *_: (0,) * nd, pipeline_mode=pl.Buffered(1))


def _inproj_prompt_kernel(x_ref, g_ref, w_ref, wft_ref, bf_ref, cw_ref, qg_ref, kg_ref, og_ref, seg_ref,
                          ycn_ref, qb_ref, kb_ref, vb_ref, kout_ref, vout_ref, lft_ref, ct_ref, cs_ref,
                          ucar_ref, ccar_ref, *, dc, da):
    j = pl.program_id(1)
    tm = x_ref.shape[1]

    @pl.when(j == 0)
    def _():
        ucar_ref[...] = jnp.zeros_like(ucar_ref)
        ccar_ref[...] = jnp.zeros_like(ccar_ref)

    x = x_ref[0]
    xn = (x * _rms_scale(x) * g_ref[...]).astype(BF16)
    seg = seg_ref[...]

    def proj(c0, n):
        return jnp.dot(xn, w_ref[:, c0:c0 + n], preferred_element_type=F32)

    ub = proj(0, dc)
    u = proj(dc, dc) * proj(2 * dc, dc)
    prev = ucar_ref[...]
    row = lax.broadcasted_iota(jnp.int32, (tm, 1), 0)
    u1 = jnp.where(row == 0, prev[7:8], pltpu.roll(u, 1, 0))
    u2 = jnp.where(row == 0, prev[6:7], jnp.where(row == 1, prev[7:8], pltpu.roll(u, 2, 0)))
    yc = ub * (cw_ref[0:1] * u2 + cw_ref[1:2] * u1 + cw_ref[2:3] * u)
    ycn_ref[0] = _head_norm(yc, og_ref[...], seg).astype(BF16)
    ucar_ref[...] = u[tm - 8:tm]
    cs_ref[0] = u[tm - (CONV_WIDTH - 1):tm]

    q = _head_norm(proj(3 * dc, da), qg_ref[...], seg)
    qb_ref[0] = (q * (HEAD_DIM ** -0.5 * LOG2E)).astype(BF16)
    k = _head_norm(proj(3 * dc + da, da), kg_ref[...], seg)
    kout_ref[0] = k
    kb_ref[0] = k.astype(BF16)
    v = proj(3 * dc + 2 * da, da)
    vout_ref[0] = v
    vb_ref[0] = v.astype(BF16)

    nh = lft_ref.shape[1]
    lf = _log_sigmoid(_dotT(wft_ref[...], xn)[:nh] + bf_ref[...])
    lft_ref[0] = lf
    lane = lax.broadcasted_iota(jnp.int32, lf.shape, 1)
    c = lf
    sh = 1
    while sh < tm:
        c = c + jnp.where(lane >= sh, pltpu.roll(c, sh, 1), 0.0)
        sh *= 2
    c = c + ccar_ref[:, 0:1]
    ct_ref[0] = c * LOG2E
    ccar_ref[...] = jnp.broadcast_to(c[:, tm - 1:tm], ccar_ref.shape)


def _inproj_prompt(x, g, w, wft, bfc, cw, qg, kg, og, seg, *, tm):
    b, s, d = x.shape
    dc = cw.shape[1]
    da = qg.shape[1]
    nh = bfc.shape[0]
    grid = (b, s // tm)
    row_blk = lambda n: pl.BlockSpec((1, tm, n), lambda i, j: (i, j, 0))
    out_shape = (
        jax.ShapeDtypeStruct((b, s, dc), BF16),
        jax.ShapeDtypeStruct((b, s, da), BF16),
        jax.ShapeDtypeStruct((b, s, da), BF16),
        jax.ShapeDtypeStruct((b, s, da), BF16),
        jax.ShapeDtypeStruct((b, s, da), F32),
        jax.ShapeDtypeStruct((b, s, da), F32),
        jax.ShapeDtypeStruct((b, nh, s), F32),
        jax.ShapeDtypeStruct((b, nh, s), F32),
        jax.ShapeDtypeStruct((b, CONV_WIDTH - 1, dc), F32),
    )
    out_specs = (
        row_blk(dc), row_blk(da), row_blk(da), row_blk(da), row_blk(da), row_blk(da),
        pl.BlockSpec((1, nh, tm), lambda i, j: (i, 0, j)),
        pl.BlockSpec((1, nh, tm), lambda i, j: (i, 0, j)),
        pl.BlockSpec((1, CONV_WIDTH - 1, dc), lambda i, j: (i, 0, 0)),
    )
    in_specs = [
        row_blk(d), _const_spec(g.shape), _const_spec(w.shape), _const_spec(wft.shape),
        _const_spec(bfc.shape), _const_spec(cw.shape), _const_spec(qg.shape), _const_spec(kg.shape),
        _const_spec(og.shape), _const_spec(seg.shape),
    ]
    return pl.pallas_call(
        functools.partial(_inproj_prompt_kernel, dc=dc, da=da),
        grid=grid, in_specs=in_specs, out_specs=out_specs, out_shape=out_shape,
        scratch_shapes=[pltpu.VMEM((8, dc), F32), pltpu.VMEM((nh, LANES), F32)],
        compiler_params=pltpu.CompilerParams(
            dimension_semantics=("arbitrary", "arbitrary"), vmem_limit_bytes=VMEM_LIMIT),
        name="inproj_prompt",
    )(x, g, w, wft, bfc, cw, qg, kg, og, seg)


def _attn_prompt_kernel(q_ref, k_ref, v_ref, c_ref, og_ref, seg_ref, o_ref, m_ref, l_ref, acc_ref, *, tk):
    hp = pl.program_id(1)
    qi = pl.program_id(2)
    tq = q_ref.shape[1]
    rep = tk // LANES
    q2 = q_ref[0]
    low = lax.broadcasted_iota(jnp.int32, (1, LANES), 1) < HEAD_DIM
    zero = jnp.zeros_like(q2)
    qh = (jnp.where(low, q2, zero), jnp.where(low, zero, q2))
    m_ref[...] = jnp.full_like(m_ref, -jnp.inf)
    l_ref[...] = jnp.zeros_like(l_ref)
    acc_ref[...] = jnp.zeros_like(acc_ref)

    def tile(ki, masked):
        k0 = pl.multiple_of(ki * tk, tk)
        k = k_ref[0, pl.ds(k0, tk), :]
        v = v_ref[0, pl.ds(k0, tk), :]
        if masked:
            keep = (lax.broadcasted_iota(jnp.int32, (tq, tk), 1)
                    <= lax.broadcasted_iota(jnp.int32, (tq, tk), 0))
        for hh in range(2):
            s = _dotT(qh[hh], k) - c_ref[0, 2 * hp + hh, pl.ds(ki, 1), :]
            if masked:
                s = jnp.where(keep, s, -jnp.inf)
            m_old = m_ref[hh]
            m_new = jnp.maximum(m_old, jnp.max(s, axis=1, keepdims=True))
            p = jnp.exp2(s - jnp.tile(m_new, (1, rep)))
            alpha = jnp.exp2(m_old - m_new)
            l_ref[hh] = alpha * l_ref[hh] + jnp.sum(p, axis=1, keepdims=True)
            acc_ref[hh] = alpha * acc_ref[hh] + jnp.dot(p.astype(BF16), v, preferred_element_type=F32)
            m_ref[hh] = m_new

    def body(ki, carry):
        tile(ki, False)
        return carry

    lax.fori_loop(0, qi, body, 0)
    tile(qi, True)
    o = jnp.where(low, acc_ref[0] / l_ref[0], acc_ref[1] / l_ref[1])
    ms = jnp.dot((o * o).astype(BF16), seg_ref[...], preferred_element_type=F32)
    o_ref[0] = (o * lax.rsqrt(ms + EPS) * og_ref[...]).astype(BF16)


def _attn_prompt(qb, kb, vb, ct4, og, seg, *, tq):
    b, s, da = qb.shape
    nh, nk, tk = ct4.shape[1:]
    assert tk == tq
    grid = (b, da // LANES, s // tq)
    in_specs = [
        pl.BlockSpec((1, tq, LANES), lambda i, h, j: (i, j, h)),
        pl.BlockSpec((1, s, LANES), lambda i, h, j: (i, 0, h)),
        pl.BlockSpec((1, s, LANES), lambda i, h, j: (i, 0, h)),
        pl.BlockSpec((1, nh, nk, tk), lambda i, h, j: (i, 0, 0, 0)),
        pl.BlockSpec((1, LANES), lambda i, h, j: (0, h)),
        _const_spec(seg.shape),
    ]
    return pl.pallas_call(
        functools.partial(_attn_prompt_kernel, tk=tk),
        grid=grid, in_specs=in_specs,
        out_specs=pl.BlockSpec((1, tq, LANES), lambda i, h, j: (i, j, h)),
        out_shape=jax.ShapeDtypeStruct((b, s, da), BF16),
        scratch_shapes=[pltpu.VMEM((2, tq, LANES), F32), pltpu.VMEM((2, tq, LANES), F32),
                        pltpu.VMEM((2, tq, LANES), F32)],
        compiler_params=pltpu.CompilerParams(
            dimension_semantics=("arbitrary", "arbitrary", "arbitrary"), vmem_limit_bytes=VMEM_LIMIT),
        name="attn_prompt",
    )(qb, kb, vb, ct4, og, seg)


def _mix_ffn_kernel(x_ref, yc_ref, ya_ref, wo_ref, fg_ref, wg_ref, wu_ref, wd_ref, o_ref, *, ff_chunk):
    dc = yc_ref.shape[1]
    mix = jnp.dot(yc_ref[...], wo_ref[0:dc, :], preferred_element_type=F32)
    mix = mix + jnp.dot(ya_ref[...], wo_ref[dc:, :], preferred_element_type=F32)
    h = x_ref[...] + mix
    hn = (h * _rms_scale(h) * fg_ref[...]).astype(BF16)
    dff = wg_ref.shape[1]
    acc = None
    for c0 in range(0, dff, ff_chunk):
        gate = jnp.dot(hn, wg_ref[:, c0:c0 + ff_chunk], preferred_element_type=F32)
        up = jnp.dot(hn, wu_ref[:, c0:c0 + ff_chunk], preferred_element_type=F32)
        a = (gate * jax.nn.sigmoid(gate) * up).astype(BF16)
        part = jnp.dot(a, wd_ref[c0:c0 + ff_chunk, :], preferred_element_type=F32)
        acc = part if acc is None else acc + part
    o_ref[...] = h + acc


def _mix_ffn(x, yc, ya, wo, fg, wg, wu, wd, *, tm, ff_chunk):
    r, d = x.shape
    row_blk = lambda n: pl.BlockSpec((tm, n), lambda i: (i, 0))
    in_specs = [row_blk(d), row_blk(yc.shape[1]), row_blk(ya.shape[1]),
                _const_spec(wo.shape), _const_spec(fg.shape), _const_spec(wg.shape),
                _const_spec(wu.shape), _const_spec(wd.shape)]
    return pl.pallas_call(
        functools.partial(_mix_ffn_kernel, ff_chunk=ff_chunk),
        grid=(r // tm,), in_specs=in_specs, out_specs=row_blk(d),
        out_shape=jax.ShapeDtypeStruct((r, d), F32),
        compiler_params=pltpu.CompilerParams(
            dimension_semantics=("arbitrary",), vmem_limit_bytes=VMEM_LIMIT),
        name="mix_ffn",
    )(x, yc, ya, wo, fg, wg, wu, wd)


def _inproj_sample_kernel(x_ref, g_ref, w_ref, wft_ref, bf_ref, cw_ref, qg_ref, kg_ref, og_ref, seg_ref,
                          st_ref, ycn_ref, q_ref, k_ref, v_ref, lft_ref, gt_ref, cs_ref, *, dc, da, nt):
    nb = x_ref.shape[0] // nt
    x = x_ref[...]
    xn = (x * _rms_scale(x) * g_ref[...]).astype(BF16)
    seg = seg_ref[...]

    def proj(c0, n):
        return jnp.dot(xn, w_ref[:, c0:c0 + n], preferred_element_type=F32)

    ub = proj(0, dc)
    u = proj(dc, dc) * proj(2 * dc, dc)
    ucat = [st_ref[i] for i in range(CONV_WIDTH - 1)] + [u[t * nb:(t + 1) * nb] for t in range(nt)]
    conv = jnp.concatenate(
        [sum(cw_ref[jj:jj + 1] * ucat[t + jj] for jj in range(CONV_WIDTH)) for t in range(nt)], axis=0)
    ycn_ref[...] = _head_norm(ub * conv, og_ref[...], seg).astype(BF16)
    for i in range(CONV_WIDTH - 1):
        cs_ref[i] = ucat[nt + i]

    q_ref[...] = _head_norm(proj(3 * dc, da), qg_ref[...], seg) * (HEAD_DIM ** -0.5)
    k_ref[...] = _head_norm(proj(3 * dc + da, da), kg_ref[...], seg)
    v_ref[...] = proj(3 * dc + 2 * da, da)

    nh = lft_ref.shape[0]
    lf = _log_sigmoid(_dotT(wft_ref[...], xn)[:nh] + bf_ref[...])
    lft_ref[...] = lf
    run = lf[:, 0:nb]
    gs = [run]
    for t in range(1, nt):
        run = run + lf[:, t * nb:(t + 1) * nb]
        gs.append(run)
    gt_ref[...] = jnp.concatenate(gs, axis=1)


def _inproj_sample(x, g, w, wft, bfc, cw, qg, kg, og, seg, st, *, nt):
    r, d = x.shape
    dc = cw.shape[1]
    da = qg.shape[1]
    nh = bfc.shape[0]
    args = (x, g, w, wft, bfc, cw, qg, kg, og, seg, st)
    out_shape = (
        jax.ShapeDtypeStruct((r, dc), BF16),
        jax.ShapeDtypeStruct((r, da), F32),
        jax.ShapeDtypeStruct((r, da), F32),
        jax.ShapeDtypeStruct((r, da), F32),
        jax.ShapeDtypeStruct((nh, r), F32),
        jax.ShapeDtypeStruct((nh, r), F32),
        jax.ShapeDtypeStruct(st.shape, F32),
    )
    full = lambda a: pl.BlockSpec(a.shape, lambda i: (0,) * len(a.shape))
    return pl.pallas_call(
        functools.partial(_inproj_sample_kernel, dc=dc, da=da, nt=nt),
        grid=(1,), in_specs=[full(a) for a in args], out_specs=tuple(full(o) for o in out_shape),
        out_shape=out_shape,
        compiler_params=pltpu.CompilerParams(
            dimension_semantics=("arbitrary",), vmem_limit_bytes=VMEM_LIMIT),
        name="inproj_sample",
    )(*args)


def _attn_decode_kernel(pt_ref, qbd_ref, kn_ref, vn_ref, gneg_ref, og_ref, seg_ref, *rest, pps, nt, nh):
    k_refs = rest[0:pps]
    v_refs = rest[pps:2 * pps]
    lf_refs = rest[2 * pps:3 * pps]
    o_ref, m_ref, l_ref, acc_ref, run_ref = rest[3 * pps:]
    step = pl.program_id(1)
    qbd = qbd_ref[0]
    nr = qbd.shape[0]

    @pl.when(step == 0)
    def _():
        sn = _dotT(qbd, kn_ref[0]) + gneg_ref[0]
        r_i = lax.broadcasted_iota(jnp.int32, sn.shape, 0)
        c_i = lax.broadcasted_iota(jnp.int32, sn.shape, 1)
        sn = jnp.where(c_i * nh <= r_i, sn, -jnp.inf)
        m0 = jnp.max(sn, axis=1, keepdims=True)
        p0 = jnp.exp(sn - m0)
        m_ref[...] = m0
        l_ref[...] = jnp.sum(p0, axis=1, keepdims=True)
        acc_ref[...] = jnp.dot(p0.astype(BF16), vn_ref[0], preferred_element_type=F32)
        run_ref[...] = jnp.zeros_like(run_ref)

    lane = lax.broadcasted_iota(jnp.int32, (nh, PAGE_SIZE), 1)
    run = run_ref[...]
    scores = []
    for i in range(pps):
        lf = lf_refs[i][0]
        suf = lf
        sh = 1
        while sh < PAGE_SIZE:
            suf = suf + jnp.where(lane < PAGE_SIZE - sh, pltpu.roll(suf, PAGE_SIZE - sh, 1), 0.0)
            sh *= 2
        decay = run + (suf - lf)
        run = run + suf[:, 0:1]
        scores.append(jnp.dot(qbd, k_refs[i][0].astype(BF16), preferred_element_type=F32)
                      + jnp.concatenate([decay] * nt, axis=0))
    run_ref[...] = run
    s = jnp.concatenate(scores, axis=1)
    m_old = m_ref[...]
    m_new = jnp.maximum(m_old, jnp.max(s, axis=1, keepdims=True))
    alpha = jnp.exp(m_old - m_new)
    p = jnp.exp(s - m_new)
    l_ref[...] = alpha * l_ref[...] + jnp.sum(p, axis=1, keepdims=True)
    pv = None
    for i in range(pps):
        part = _dotT(p[:, i * PAGE_SIZE:(i + 1) * PAGE_SIZE].astype(BF16), v_refs[i][0].astype(BF16))
        pv = part if pv is None else pv + part
    acc_ref[...] = alpha * acc_ref[...] + pv
    m_ref[...] = m_new

    @pl.when(step == pl.num_programs(1) - 1)
    def _():
        o = acc_ref[...] / l_ref[...]
        r_i = lax.broadcasted_iota(jnp.int32, o.shape, 0)
        c_i = lax.broadcasted_iota(jnp.int32, o.shape, 1)
        o = jnp.where(c_i // HEAD_DIM == r_i % nh, o, 0.0)
        y = jnp.concatenate([jnp.sum(o[t * nh:(t + 1) * nh], axis=0, keepdims=True) for t in range(nt)], axis=0)
        o_ref[0] = _head_norm(y, og_ref[...], seg_ref[...])


def _attn_decode(page_table, qbd, kn, vn, gneg, og, seg, ck, cv, clf, *, pps, nt, nh):
    nb, npages = page_table.shape
    nr, da = qbd.shape[1:]
    steps = npages // pps

    def page_map(i):
        return lambda b, s, pt: (pt[b, npages - 1 - (s * pps + i)], 0, 0)

    per_b = lambda a: pl.BlockSpec((1,) + a.shape[1:], lambda b, s, pt: (b,) + (0,) * (a.ndim - 1))
    const = lambda a: pl.BlockSpec(a.shape, lambda b, s, pt: (0,) * a.ndim)
    in_specs = [per_b(qbd), per_b(kn), per_b(vn), per_b(gneg), const(og), const(seg)]
    in_specs += [pl.BlockSpec((1, da, PAGE_SIZE), page_map(i)) for i in range(pps)]
    in_specs += [pl.BlockSpec((1, da, PAGE_SIZE), page_map(i)) for i in range(pps)]
    in_specs += [pl.BlockSpec((1, nh, PAGE_SIZE), page_map(i)) for i in range(pps)]
    grid_spec = pltpu.PrefetchScalarGridSpec(
        num_scalar_prefetch=1, grid=(nb, steps), in_specs=in_specs,
        out_specs=pl.BlockSpec((1, nt, da), lambda b, s, pt: (b, 0, 0)),
        scratch_shapes=[pltpu.VMEM((nr, 1), F32), pltpu.VMEM((nr, 1), F32),
                        pltpu.VMEM((nr, da), F32), pltpu.VMEM((nh, 1), F32)])
    return pl.pallas_call(
        functools.partial(_attn_decode_kernel, pps=pps, nt=nt, nh=nh),
        grid_spec=grid_spec,
        out_shape=jax.ShapeDtypeStruct((nb, nt, da), F32),
        compiler_params=pltpu.CompilerParams(
            dimension_semantics=("arbitrary", "arbitrary"), vmem_limit_bytes=VMEM_LIMIT),
        name="attn_decode",
    )(page_table, qbd, kn, vn, gneg, og, seg, *([ck] * pps), *([cv] * pps), *([clf] * pps))


def _pick(n, pref):
    t = min(n, pref)
    while n % t:
        t //= 2
    return t


def _layer(xp, xs, cache_k, cache_v, cache_logf, state_conv, page_table,
           norm_g, w_in, b_f, conv_w, q_g, k_g, out_g, w_out, ffn_g, w_gate, w_up, w_down):
    bsz, seq, d = xp.shape
    nb, nt, _ = xs.shape
    dc = conv_w.shape[1]
    nh = b_f.shape[0]
    da = nh * HEAD_DIM
    npages = page_table.shape[1]

    g = norm_g.reshape(1, d)
    w_main = w_in[:, :3 * dc + 3 * da].astype(BF16)
    wft = jnp.zeros((16, d), BF16).at[:nh].set(w_in[:, 3 * dc + 3 * da:].T.astype(BF16))
    bfc = b_f.reshape(nh, 1)
    qg = jnp.tile(q_g, nh).reshape(1, da)
    kg = jnp.tile(k_g, nh).reshape(1, da)
    og_c = out_g[:dc].reshape(1, dc)
    og_a = out_g[dc:].reshape(1, da)
    gi = jnp.arange(MXU_DIM) // HEAD_DIM
    seg = jnp.where(gi[:, None] == gi[None, :], 1.0 / HEAD_DIM, 0.0).astype(BF16)
    wo = w_out.astype(BF16)
    fg = ffn_g.reshape(1, d)
    wg = w_gate.astype(BF16)
    wu = w_up.astype(BF16)
    wd = w_down.astype(BF16)
    dff = wg.shape[1]

    tm = _pick(seq, 512)
    ycn, qb, kb, vb, k_out, v_out, lft, ct, cs_p = _inproj_prompt(
        xp, g, w_main, wft, bfc, conv_w, qg, kg, og_c, seg, tm=tm)
    tq = _pick(seq, 512)
    yan = _attn_prompt(qb, kb, vb, ct.reshape(bsz, nh, seq // tq, tq), og_a, seg[:LANES, :LANES], tq=tq)
    rows = bsz * seq
    yp = _mix_ffn(xp.reshape(rows, d), ycn.reshape(rows, dc), yan.reshape(rows, da),
                  wo, fg, wg, wu, wd, tm=_pick(rows, 512), ff_chunk=dff)
    y_prompt = yp.reshape(bsz, seq, d)
    nk_p = k_out.reshape(bsz, seq, nh, HEAD_DIM)
    nv_p = v_out.reshape(bsz, seq, nh, HEAD_DIM)
    nl_p = jnp.transpose(lft, (0, 2, 1))

    rs = nb * nt
    xs_tm = jnp.transpose(xs, (1, 0, 2)).reshape(rs, d)
    st = jnp.transpose(state_conv, (1, 0, 2))
    ycn_s, q_s, k_s, v_s, lft_s, gt_s, cs_s = _inproj_sample(
        xs_tm, g, w_main, wft, bfc, conv_w, qg, kg, og_c, seg, st, nt=nt)

    to_bt = lambda a: jnp.transpose(a.reshape(nt, nb, -1), (1, 0, 2))
    eye = jnp.eye(nh, dtype=F32)
    qbd = jnp.einsum('tbhd,hg->bthgd', q_s.reshape(nt, nb, nh, HEAD_DIM), eye)
    qbd = qbd.reshape(nb, nt * nh, da).astype(BF16)
    pad_rows = 16 - nt
    kn = jnp.pad(to_bt(k_s), ((0, 0), (0, pad_rows), (0, 0))).astype(BF16)
    vn = jnp.pad(to_bt(v_s), ((0, 0), (0, pad_rows), (0, 0))).astype(BF16)
    gneg = -jnp.transpose(gt_s.reshape(nh, nt, nb), (2, 0, 1))
    gneg = jnp.pad(gneg, ((0, 0), (0, 0), (0, 16 - nt)))
    gneg = jnp.tile(gneg, (1, nt, 1))
    nphys = cache_k.shape[0]
    ck = jnp.transpose(cache_k, (0, 2, 3, 1)).reshape(nphys, da, PAGE_SIZE)
    cv = jnp.transpose(cache_v, (0, 2, 3, 1)).reshape(nphys, da, PAGE_SIZE)
    clf = jnp.transpose(cache_logf, (0, 2, 1))
    pps = _pick(npages, 16)
    yan_s = _attn_decode(page_table, qbd, kn, vn, gneg, og_a, seg, ck, cv, clf, pps=pps, nt=nt, nh=nh)
    ys = _mix_ffn(xs.reshape(rs, d), to_bt(ycn_s).reshape(rs, dc), yan_s.reshape(rs, da).astype(BF16),
                  wo, fg, wg, wu, wd, tm=_pick(rs, 512), ff_chunk=dff)
    y_sample = ys.reshape(nb, nt, d)
    nk_s = to_bt(k_s).reshape(nb, nt, nh, HEAD_DIM)
    nv_s = to_bt(v_s).reshape(nb, nt, nh, HEAD_DIM)
    nl_s = jnp.transpose(lft_s.reshape(nh, nt, nb), (2, 1, 0))
    cs_s = jnp.transpose(cs_s, (1, 0, 2))
    return (y_prompt, y_sample, nk_p, nv_p, nl_p, cs_p, nk_s, nv_s, nl_s, cs_s)


def kernel(x_prompt, x_sample, cache_k, cache_v, cache_logf, state_conv, page_table,
           norm_mix_g, w_in, b_f, conv_w, q_norm_g, k_norm_g, out_norm_g, w_out,
           norm_ffn_g, w_gate, w_up, w_down):
    depth = w_in.shape[0]
    xp, xs = x_prompt, x_sample
    outs = []
    for l in range(depth):
        res = _layer(xp, xs, cache_k[l], cache_v[l], cache_logf[l], state_conv[l], page_table,
                     norm_mix_g[l], w_in[l], b_f[l], conv_w[l], q_norm_g[l], k_norm_g[l], out_norm_g[l],
                     w_out[l], norm_ffn_g[l], w_gate[l], w_up[l], w_down[l])
        xp, xs = res[0], res[1]
        outs.append(res[2:])
    stacked = tuple(jnp.stack([o[i] for o in outs]) for i in range(8))
    return (xp, xs) + stacked
```

```python
import functools

import jax
import jax.numpy as jnp
from jax import lax
from jax.experimental import pallas as pl
from jax.experimental.pallas import tpu as pltpu

HEAD_DIM = 64
CONV_WIDTH = 3
PAGE_SIZE = 128
EPS = 1e-6
LOG2E = 1.4426950408889634
LANES = 128
MXU_DIM = 256
VMEM_LIMIT = 56 * 1024 * 1024
ATTN_HEADS_PER_STEP = 8

F32 = jnp.float32
BF16 = jnp.bfloat16


def _log_sigmoid(x):
    return jnp.minimum(x, 0.0) - jnp.log1p(jnp.exp(-jnp.abs(x)))


def _rms_scale(x):
    return lax.rsqrt(jnp.mean(x * x, axis=-1, keepdims=True) + EPS)


def _head_mean_sq(y, seg):
    y2 = (y * y).astype(BF16)
    parts = [jnp.dot(y2[:, c:c + MXU_DIM], seg, preferred_element_type=F32)
             for c in range(0, y.shape[1], MXU_DIM)]
    return parts[0] if len(parts) == 1 else jnp.concatenate(parts, axis=1)


def _head_norm(y, gain, seg):
    return y * lax.rsqrt(_head_mean_sq(y, seg) + EPS) * gain


def _dotT(a, b):
    return lax.dot_general(a, b, (((1,), (1,)), ((), ())), preferred_element_type=F32)


def _const_spec(shape):
    nd = len(shape)
    return pl.BlockSpec(shape, lambda *_: (0,) * nd, pipeline_mode=pl.Buffered(1))


def _split3(x):
    hi = x.astype(BF16)
    r1 = x - hi.astype(F32)
    mid = r1.astype(BF16)
    lo = (r1 - mid.astype(F32)).astype(BF16)
    return hi, mid, lo


def _store_head_major(y, extra, o_ref):
    low = lax.broadcasted_iota(jnp.int32, (1, LANES), 1) < HEAD_DIM
    for h in range(o_ref.shape[1]):
        pair = y[:, (h // 2) * LANES:(h // 2 + 1) * LANES]
        if h % 2:
            pair = pltpu.roll(pair, HEAD_DIM, 1)
        o_ref[0, h] = jnp.where(low, pair, extra(h)).astype(BF16)


def _inproj_prompt_kernel(x_ref, g_ref, w_ref, wft_ref, bf_ref, cw_ref, qg_ref, kg_ref, og_ref, seg_ref, eye_ref,
                          ycn_ref, qa_ref, ka_ref, va_ref, kout_ref, vout_ref, lft_ref, cs_ref,
                          ucar_ref, ccar_ref, *, dc, da):
    j = pl.program_id(1)
    tm = x_ref.shape[1]
    nh = lft_ref.shape[1]

    @pl.when(j == 0)
    def _():
        ucar_ref[...] = jnp.zeros_like(ucar_ref)
        ccar_ref[...] = jnp.zeros_like(ccar_ref)

    x = x_ref[0]
    xn = (x * _rms_scale(x) * g_ref[...]).astype(BF16)
    seg = seg_ref[...]

    def proj(c0, n):
        return jnp.dot(xn, w_ref[:, c0:c0 + n], preferred_element_type=F32)

    lf = _log_sigmoid(_dotT(wft_ref[...], xn)[:nh] + bf_ref[...])
    lft_ref[0] = lf
    lane_t = lax.broadcasted_iota(jnp.int32, lf.shape, 1)
    c = lf
    sh = 1
    while sh < tm:
        c = c + jnp.where(lane_t >= sh, pltpu.roll(c, sh, 1), 0.0)
        sh *= 2
    c = c + ccar_ref[:, 0:1]
    ccar_ref[...] = jnp.broadcast_to(c[:, tm - 1:tm], ccar_ref.shape)
    parts = _split3(c * (-LOG2E))
    c3 = jnp.concatenate([jnp.zeros((HEAD_DIM, tm), BF16), *parts,
                          jnp.zeros((LANES - HEAD_DIM - 3 * nh, tm), BF16)], axis=0)
    bias_cols = _dotT(eye_ref[...], c3)

    ub = proj(0, dc)
    u = proj(dc, dc) * proj(2 * dc, dc)
    prev = ucar_ref[...]
    row = lax.broadcasted_iota(jnp.int32, (tm, 1), 0)
    u1 = jnp.where(row == 0, prev[7:8], pltpu.roll(u, 1, 0))
    u2 = jnp.where(row == 0, prev[6:7], jnp.where(row == 1, prev[7:8], pltpu.roll(u, 2, 0)))
    yc = ub * (cw_ref[0:1] * u2 + cw_ref[1:2] * u1 + cw_ref[2:3] * u)
    ycn_ref[0] = _head_norm(yc, og_ref[...], seg).astype(BF16)
    ucar_ref[...] = u[tm - 8:tm]
    cs_ref[0] = u[tm - (CONV_WIDTH - 1):tm]

    lane = lax.broadcasted_iota(jnp.int32, (1, LANES), 1)
    q = _head_norm(proj(3 * dc, da), qg_ref[...], seg) * (HEAD_DIM ** -0.5 * LOG2E)
    _store_head_major(
        q, lambda h: jnp.where((lane >= HEAD_DIM) & (lane < HEAD_DIM + 3 * nh) & (lane % nh == h), 1.0, 0.0), qa_ref)
    k = _head_norm(proj(3 * dc + da, da), kg_ref[...], seg)
    kout_ref[0] = k
    _store_head_major(k, lambda h: bias_cols, ka_ref)
    v = proj(3 * dc + 2 * da, da)
    vout_ref[0] = v
    _store_head_major(v, lambda h: jnp.where(lane == HEAD_DIM, 1.0, 0.0), va_ref)


def _inproj_prompt(x, g, w, wft, bfc, cw, qg, kg, og, seg, eye, *, tm):
    b, s, d = x.shape
    dc = cw.shape[1]
    da = qg.shape[1]
    nh = bfc.shape[0]
    grid = (b, s // tm)
    row_blk = lambda n: pl.BlockSpec((1, tm, n), lambda i, j: (i, j, 0))
    head_blk = pl.BlockSpec((1, nh, tm, LANES), lambda i, j: (i, 0, j, 0))
    out_shape = (
        jax.ShapeDtypeStruct((b, s, dc), BF16),
        jax.ShapeDtypeStruct((b, nh, s, LANES), BF16),
        jax.ShapeDtypeStruct((b, nh, s, LANES), BF16),
        jax.ShapeDtypeStruct((b, nh, s, LANES), BF16),
        jax.ShapeDtypeStruct((b, s, da), F32),
        jax.ShapeDtypeStruct((b, s, da), F32),
        jax.ShapeDtypeStruct((b, nh, s), F32),
        jax.ShapeDtypeStruct((b, CONV_WIDTH - 1, dc), F32),
    )
    out_specs = (
        row_blk(dc), head_blk, head_blk, head_blk, row_blk(da), row_blk(da),
        pl.BlockSpec((1, nh, tm), lambda i, j: (i, 0, j)),
        pl.BlockSpec((1, CONV_WIDTH - 1, dc), lambda i, j: (i, 0, 0)),
    )
    in_specs = [
        row_blk(d), _const_spec(g.shape), _const_spec(w.shape), _const_spec(wft.shape),
        _const_spec(bfc.shape), _const_spec(cw.shape), _const_spec(qg.shape), _const_spec(kg.shape),
        _const_spec(og.shape), _const_spec(seg.shape), _const_spec(eye.shape),
    ]
    return pl.pallas_call(
        functools.partial(_inproj_prompt_kernel, dc=dc, da=da),
        grid=grid, in_specs=in_specs, out_specs=out_specs, out_shape=out_shape,
        scratch_shapes=[pltpu.VMEM((8, dc), F32), pltpu.VMEM((nh, LANES), F32)],
        compiler_params=pltpu.CompilerParams(
            dimension_semantics=("arbitrary", "arbitrary"), vmem_limit_bytes=VMEM_LIMIT),
        name="inproj_prompt",
    )(x, g, w, wft, bfc, cw, qg, kg, og, seg, eye)


def _attn_prompt_kernel(q_ref, k_ref, v_ref, og_ref, seg_ref, o_ref, m_ref, acc_ref, *, tk):
    qi = pl.program_id(2)
    nhs, tq = q_ref.shape[1:3]
    rep = tk // LANES
    m_ref[...] = jnp.full_like(m_ref, -jnp.inf)
    acc_ref[...] = jnp.zeros_like(acc_ref)

    def tile(ki, masked):
        k0 = pl.multiple_of(ki * tk, tk)
        if masked:
            keep = (lax.broadcasted_iota(jnp.int32, (tq, tk), 1)
                    <= lax.broadcasted_iota(jnp.int32, (tq, tk), 0))
        for hh in range(nhs):
            s = _dotT(q_ref[0, hh], k_ref[0, hh, pl.ds(k0, tk), :])
            if masked:
                s = jnp.where(keep, s, -jnp.inf)
            m_old = m_ref[hh]
            m_new = jnp.maximum(m_old, jnp.max(s, axis=1, keepdims=True))
            p = jnp.exp2(s - jnp.tile(m_new, (1, rep)))
            alpha = jnp.exp2(m_old - m_new)
            acc_ref[hh] = alpha * acc_ref[hh] + jnp.dot(
                p.astype(BF16), v_ref[0, hh, pl.ds(k0, tk), :], preferred_element_type=F32)
            m_ref[hh] = m_new

    def body(ki, carry):
        tile(ki, False)
        return carry

    lax.fori_loop(0, qi, body, 0)
    tile(qi, True)
    low = lax.broadcasted_iota(jnp.int32, (1, LANES), 1) < HEAD_DIM
    seg = seg_ref[...]
    for hp in range(nhs // 2):
        outs = []
        for hh in (2 * hp, 2 * hp + 1):
            a = acc_ref[hh]
            outs.append(a / a[:, HEAD_DIM:HEAD_DIM + 1])
        o = jnp.where(low, outs[0], pltpu.roll(outs[1], HEAD_DIM, 1))
        ms = jnp.dot((o * o).astype(BF16), seg, preferred_element_type=F32)
        o_ref[0, :, hp * LANES:(hp + 1) * LANES] = (
            o * lax.rsqrt(ms + EPS) * og_ref[:, hp * LANES:(hp + 1) * LANES]).astype(BF16)


def _attn_prompt(qa, ka, va, og, seg, *, tq, nhs):
    b, nh, s, _ = qa.shape
    grid = (b, nh // nhs, s // tq)
    wo = nhs * HEAD_DIM
    in_specs = [
        pl.BlockSpec((1, nhs, tq, LANES), lambda i, h, j: (i, h, j, 0)),
        pl.BlockSpec((1, nhs, s, LANES), lambda i, h, j: (i, h, 0, 0)),
        pl.BlockSpec((1, nhs, s, LANES), lambda i, h, j: (i, h, 0, 0)),
        pl.BlockSpec((1, wo), lambda i, h, j: (0, h)),
        _const_spec(seg.shape),
    ]
    return pl.pallas_call(
        functools.partial(_attn_prompt_kernel, tk=tq),
        grid=grid, in_specs=in_specs,
        out_specs=pl.BlockSpec((1, tq, wo), lambda i, h, j: (i, j, h)),
        out_shape=jax.ShapeDtypeStruct((b, s, nh * HEAD_DIM), BF16),
        scratch_shapes=[pltpu.VMEM((nhs, tq, LANES), F32), pltpu.VMEM((nhs, tq, LANES), F32)],
        compiler_params=pltpu.CompilerParams(
            dimension_semantics=("arbitrary", "arbitrary", "arbitrary"), vmem_limit_bytes=VMEM_LIMIT),
        name="attn_prompt",
    )(qa, ka, va, og, seg)


def _mix_ffn_kernel(x_ref, yc_ref, ya_ref, wo_ref, fg_ref, wg_ref, wu_ref, wd_ref, o_ref, *, ff_chunk):
    dc = yc_ref.shape[1]
    mix = jnp.dot(yc_ref[...], wo_ref[0:dc, :], preferred_element_type=F32)
    mix = mix + jnp.dot(ya_ref[...], wo_ref[dc:, :], preferred_element_type=F32)
    h = x_ref[...] + mix
    hn = (h * _rms_scale(h) * fg_ref[...]).astype(BF16)
    dff = wg_ref.shape[1]
    acc = None
    for c0 in range(0, dff, ff_chunk):
        gate = jnp.dot(hn, wg_ref[:, c0:c0 + ff_chunk], preferred_element_type=F32)
        up = jnp.dot(hn, wu_ref[:, c0:c0 + ff_chunk], preferred_element_type=F32)
        a = (gate * jax.nn.sigmoid(gate) * up).astype(BF16)
        part = jnp.dot(a, wd_ref[c0:c0 + ff_chunk, :], preferred_element_type=F32)
        acc = part if acc is None else acc + part
    o_ref[...] = h + acc


def _mix_ffn(x, yc, ya, wo, fg, wg, wu, wd, *, tm, ff_chunk):
    r, d = x.shape
    row_blk = lambda n: pl.BlockSpec((tm, n), lambda i: (i, 0))
    in_specs = [row_blk(d), row_blk(yc.shape[1]), row_blk(ya.shape[1]),
                _const_spec(wo.shape), _const_spec(fg.shape), _const_spec(wg.shape),
                _const_spec(wu.shape), _const_spec(wd.shape)]
    return pl.pallas_call(
        functools.partial(_mix_ffn_kernel, ff_chunk=ff_chunk),
        grid=(r // tm,), in_specs=in_specs, out_specs=row_blk(d),
        out_shape=jax.ShapeDtypeStruct((r, d), F32),
        compiler_params=pltpu.CompilerParams(
            dimension_semantics=("arbitrary",), vmem_limit_bytes=VMEM_LIMIT),
        name="mix_ffn",
    )(x, yc, ya, wo, fg, wg, wu, wd)


def _inproj_sample_kernel(x_ref, g_ref, w_ref, wft_ref, bf_ref, cw_ref, qg_ref, kg_ref, og_ref, seg_ref,
                          st_ref, ycn_ref, q_ref, k_ref, v_ref, lft_ref, gt_ref, cs_ref, *, dc, da, nt):
    nb = x_ref.shape[0] // nt
    x = x_ref[...]
    xn = (x * _rms_scale(x) * g_ref[...]).astype(BF16)
    seg = seg_ref[...]

    def proj(c0, n):
        return jnp.dot(xn, w_ref[:, c0:c0 + n], preferred_element_type=F32)

    ub = proj(0, dc)
    u = proj(dc, dc) * proj(2 * dc, dc)
    ucat = [st_ref[i] for i in range(CONV_WIDTH - 1)] + [u[t * nb:(t + 1) * nb] for t in range(nt)]
    conv = jnp.concatenate(
        [sum(cw_ref[jj:jj + 1] * ucat[t + jj] for jj in range(CONV_WIDTH)) for t in range(nt)], axis=0)
    ycn_ref[...] = _head_norm(ub * conv, og_ref[...], seg).astype(BF16)
    for i in range(CONV_WIDTH - 1):
        cs_ref[i] = ucat[nt + i]

    q_ref[...] = _head_norm(proj(3 * dc, da), qg_ref[...], seg) * (HEAD_DIM ** -0.5)
    k_ref[...] = _head_norm(proj(3 * dc + da, da), kg_ref[...], seg)
    v_ref[...] = proj(3 * dc + 2 * da, da)

    nh = lft_ref.shape[0]
    lf = _log_sigmoid(_dotT(wft_ref[...], xn)[:nh] + bf_ref[...])
    lft_ref[...] = lf
    run = lf[:, 0:nb]
    gs = [run]
    for t in range(1, nt):
        run = run + lf[:, t * nb:(t + 1) * nb]
        gs.append(run)
    gt_ref[...] = jnp.concatenate(gs, axis=1)


def _inproj_sample(x, g, w, wft, bfc, cw, qg, kg, og, seg, st, *, nt):
    r, d = x.shape
    dc = cw.shape[1]
    da = qg.shape[1]
    nh = bfc.shape[0]
    args = (x, g, w, wft, bfc, cw, qg, kg, og, seg, st)
    out_shape = (
        jax.ShapeDtypeStruct((r, dc), BF16),
        jax.ShapeDtypeStruct((r, da), F32),
        jax.ShapeDtypeStruct((r, da), F32),
        jax.ShapeDtypeStruct((r, da), F32),
        jax.ShapeDtypeStruct((nh, r), F32),
        jax.ShapeDtypeStruct((nh, r), F32),
        jax.ShapeDtypeStruct(st.shape, F32),
    )
    full = lambda a: pl.BlockSpec(a.shape, lambda i: (0,) * len(a.shape))
    return pl.pallas_call(
        functools.partial(_inproj_sample_kernel, dc=dc, da=da, nt=nt),
        grid=(1,), in_specs=[full(a) for a in args], out_specs=tuple(full(o) for o in out_shape),
        out_shape=out_shape,
        compiler_params=pltpu.CompilerParams(
            dimension_semantics=("arbitrary",), vmem_limit_bytes=VMEM_LIMIT),
        name="inproj_sample",
    )(*args)


def _attn_decode_kernel(pt_ref, qbd_ref, kn_ref, vn_ref, gneg_ref, og_ref, seg_ref, *rest, pps, nt, nh):
    k_refs = rest[0:pps]
    v_refs = rest[pps:2 * pps]
    lf_refs = rest[2 * pps:3 * pps]
    o_ref, m_ref, l_ref, acc_ref, run_ref = rest[3 * pps:]
    step = pl.program_id(1)
    qbd = qbd_ref[0]
    nr = qbd.shape[0]

    @pl.when(step == 0)
    def _():
        sn = _dotT(qbd, kn_ref[0]) + gneg_ref[0]
        r_i = lax.broadcasted_iota(jnp.int32, sn.shape, 0)
        c_i = lax.broadcasted_iota(jnp.int32, sn.shape, 1)
        sn = jnp.where(c_i * nh <= r_i, sn, -jnp.inf)
        m0 = jnp.max(sn, axis=1, keepdims=True)
        p0 = jnp.exp(sn - m0)
        m_ref[...] = m0
        l_ref[...] = jnp.sum(p0, axis=1, keepdims=True)
        acc_ref[...] = jnp.dot(p0.astype(BF16), vn_ref[0], preferred_element_type=F32)
        run_ref[...] = jnp.zeros_like(run_ref)

    lane = lax.broadcasted_iota(jnp.int32, (nh, PAGE_SIZE), 1)
    run = run_ref[...]
    scores = []
    for i in range(pps):
        lf = lf_refs[i][0]
        suf = lf
        sh = 1
        while sh < PAGE_SIZE:
            suf = suf + jnp.where(lane < PAGE_SIZE - sh, pltpu.roll(suf, PAGE_SIZE - sh, 1), 0.0)
            sh *= 2
        decay = run + (suf - lf)
        run = run + suf[:, 0:1]
        scores.append(jnp.dot(qbd, k_refs[i][0].astype(BF16), preferred_element_type=F32)
                      + jnp.concatenate([decay] * nt, axis=0))
    run_ref[...] = run
    s = jnp.concatenate(scores, axis=1)
    m_old = m_ref[...]
    m_new = jnp.maximum(m_old, jnp.max(s, axis=1, keepdims=True))
    alpha = jnp.exp(m_old - m_new)
    p = jnp.exp(s - m_new)
    l_ref[...] = alpha * l_ref[...] + jnp.sum(p, axis=1, keepdims=True)
    pv = None
    for i in range(pps):
        part = _dotT(p[:, i * PAGE_SIZE:(i + 1) * PAGE_SIZE].astype(BF16), v_refs[i][0].astype(BF16))
        pv = part if pv is None else pv + part
    acc_ref[...] = alpha * acc_ref[...] + pv
    m_ref[...] = m_new

    @pl.when(step == pl.num_programs(1) - 1)
    def _():
        o = acc_ref[...] / l_ref[...]
        r_i = lax.broadcasted_iota(jnp.int32, o.shape, 0)
        c_i = lax.broadcasted_iota(jnp.int32, o.shape, 1)
        o = jnp.where(c_i // HEAD_DIM == r_i % nh, o, 0.0)
        y = jnp.concatenate([jnp.sum(o[t * nh:(t + 1) * nh], axis=0, keepdims=True) for t in range(nt)], axis=0)
        o_ref[0] = _head_norm(y, og_ref[...], seg_ref[...])


def _attn_decode(page_table, qbd, kn, vn, gneg, og, seg, ck, cv, clf, *, pps, nt, nh):
    nb, npages = page_table.shape
    nr, da = qbd.shape[1:]
    steps = npages // pps

    def page_map(i):
        return lambda b, s, pt: (pt[b, npages - 1 - (s * pps + i)], 0, 0)

    per_b = lambda a: pl.BlockSpec((1,) + a.shape[1:], lambda b, s, pt: (b,) + (0,) * (a.ndim - 1))
    const = lambda a: pl.BlockSpec(a.shape, lambda b, s, pt: (0,) * a.ndim)
    in_specs = [per_b(qbd), per_b(kn), per_b(vn), per_b(gneg), const(og), const(seg)]
    in_specs += [pl.BlockSpec((1, da, PAGE_SIZE), page_map(i)) for i in range(pps)]
    in_specs += [pl.BlockSpec((1, da, PAGE_SIZE), page_map(i)) for i in range(pps)]
    in_specs += [pl.BlockSpec((1, nh, PAGE_SIZE), page_map(i)) for i in range(pps)]
    grid_spec = pltpu.PrefetchScalarGridSpec(
        num_scalar_prefetch=1, grid=(nb, steps), in_specs=in_specs,
        out_specs=pl.BlockSpec((1, nt, da), lambda b, s, pt: (b, 0, 0)),
        scratch_shapes=[pltpu.VMEM((nr, 1), F32), pltpu.VMEM((nr, 1), F32),
                        pltpu.VMEM((nr, da), F32), pltpu.VMEM((nh, 1), F32)])
    return pl.pallas_call(
        functools.partial(_attn_decode_kernel, pps=pps, nt=nt, nh=nh),
        grid_spec=grid_spec,
        out_shape=jax.ShapeDtypeStruct((nb, nt, da), F32),
        compiler_params=pltpu.CompilerParams(
            dimension_semantics=("arbitrary", "arbitrary"), vmem_limit_bytes=VMEM_LIMIT),
        name="attn_decode",
    )(page_table, qbd, kn, vn, gneg, og, seg, *([ck] * pps), *([cv] * pps), *([clf] * pps))


def _pick(n, pref):
    t = min(n, pref)
    while n % t:
        t //= 2
    return t


def _layer(xp, xs, cache_k, cache_v, cache_logf, state_conv, page_table,
           norm_g, w_in, b_f, conv_w, q_g, k_g, out_g, w_out, ffn_g, w_gate, w_up, w_down):
    bsz, seq, d = xp.shape
    nb, nt, _ = xs.shape
    dc = conv_w.shape[1]
    nh = b_f.shape[0]
    da = nh * HEAD_DIM
    npages = page_table.shape[1]

    g = norm_g.reshape(1, d)
    w_main = w_in[:, :3 * dc + 3 * da].astype(BF16)
    wft = jnp.zeros((16, d), BF16).at[:nh].set(w_in[:, 3 * dc + 3 * da:].T.astype(BF16))
    bfc = b_f.reshape(nh, 1)
    qg = jnp.tile(q_g, nh).reshape(1, da)
    kg = jnp.tile(k_g, nh).reshape(1, da)
    og_c = out_g[:dc].reshape(1, dc)
    og_a = out_g[dc:].reshape(1, da)
    gi = jnp.arange(MXU_DIM) // HEAD_DIM
    seg = jnp.where(gi[:, None] == gi[None, :], 1.0 / HEAD_DIM, 0.0).astype(BF16)
    wo = w_out.astype(BF16)
    fg = ffn_g.reshape(1, d)
    wg = w_gate.astype(BF16)
    wu = w_up.astype(BF16)
    wd = w_down.astype(BF16)
    dff = wg.shape[1]

    tm = _pick(seq, 512)
    eye = jnp.eye(tm, dtype=BF16)
    ycn, qa, ka, va, k_out, v_out, lft, cs_p = _inproj_prompt(
        xp, g, w_main, wft, bfc, conv_w, qg, kg, og_c, seg, eye, tm=tm)
    yan = _attn_prompt(qa, ka, va, og_a, seg[:LANES, :LANES], tq=_pick(seq, 512), nhs=ATTN_HEADS_PER_STEP)
    rows = bsz * seq
    yp = _mix_ffn(xp.reshape(rows, d), ycn.reshape(rows, dc), yan.reshape(rows, da),
                  wo, fg, wg, wu, wd, tm=_pick(rows, 512), ff_chunk=dff)
    y_prompt = yp.reshape(bsz, seq, d)
    nk_p = k_out.reshape(bsz, seq, nh, HEAD_DIM)
    nv_p = v_out.reshape(bsz, seq, nh, HEAD_DIM)
    nl_p = jnp.transpose(lft, (0, 2, 1))

    rs = nb * nt
    xs_tm = jnp.transpose(xs, (1, 0, 2)).reshape(rs, d)
    st = jnp.transpose(state_conv, (1, 0, 2))
    ycn_s, q_s, k_s, v_s, lft_s, gt_s, cs_s = _inproj_sample(
        xs_tm, g, w_main, wft, bfc, conv_w, qg, kg, og_c, seg, st, nt=nt)

    to_bt = lambda a: jnp.transpose(a.reshape(nt, nb, -1), (1, 0, 2))
    eye_h = jnp.eye(nh, dtype=F32)
    qbd = jnp.einsum('tbhd,hg->bthgd', q_s.reshape(nt, nb, nh, HEAD_DIM), eye_h)
    qbd = qbd.reshape(nb, nt * nh, da).astype(BF16)
    pad_rows = 16 - nt
    kn = jnp.pad(to_bt(k_s), ((0, 0), (0, pad_rows), (0, 0))).astype(BF16)
    vn = jnp.pad(to_bt(v_s), ((0, 0), (0, pad_rows), (0, 0))).astype(BF16)
    gneg = -jnp.transpose(gt_s.reshape(nh, nt, nb), (2, 0, 1))
    gneg = jnp.pad(gneg, ((0, 0), (0, 0), (0, 16 - nt)))
    gneg = jnp.tile(gneg, (1, nt, 1))
    nphys = cache_k.shape[0]
    ck = jnp.transpose(cache_k, (0, 2, 3, 1)).reshape(nphys, da, PAGE_SIZE)
    cv = jnp.transpose(cache_v, (0, 2, 3, 1)).reshape(nphys, da, PAGE_SIZE)
    clf = jnp.transpose(cache_logf, (0, 2, 1))
    pps = _pick(npages, 16)
    yan_s = _attn_decode(page_table, qbd, kn, vn, gneg, og_a, seg, ck, cv, clf, pps=pps, nt=nt, nh=nh)
    ys = _mix_ffn(xs.reshape(rs, d), to_bt(ycn_s).reshape(rs, dc), yan_s.reshape(rs, da).astype(BF16),
                  wo, fg, wg, wu, wd, tm=_pick(rs, 512), ff_chunk=dff)
    y_sample = ys.reshape(nb, nt, d)
    nk_s = to_bt(k_s).reshape(nb, nt, nh, HEAD_DIM)
    nv_s = to_bt(v_s).reshape(nb, nt, nh, HEAD_DIM)
    nl_s = jnp.transpose(lft_s.reshape(nh, nt, nb), (2, 1, 0))
    cs_s = jnp.transpose(cs_s, (1, 0, 2))
    return (y_prompt, y_sample, nk_p, nv_p, nl_p, cs_p, nk_s, nv_s, nl_s, cs_s)


def kernel(x_prompt, x_sample, cache_k, cache_v, cache_logf, state_conv, page_table,
           norm_mix_g, w_in, b_f, conv_w, q_norm_g, k_norm_g, out_norm_g, w_out,
           norm_ffn_g, w_gate, w_up, w_down):
    depth = w_in.shape[0]
    xp, xs = x_prompt, x_sample
    outs = []
    for l in range(depth):
        res = _layer(xp, xs, cache_k[l], cache_v[l], cache_logf[l], state_conv[l], page_table,
                     norm_mix_g[l], w_in[l], b_f[l], conv_w[l], q_norm_g[l], k_norm_g[l], out_norm_g[l],
                     w_out[l], norm_ffn_g[l], w_gate[l], w_up[l], w_down[l])
        xp, xs = res[0], res[1]
        outs.append(res[2:])
    stacked = tuple(jnp.stack([o[i] for o in outs]) for i in range(8))
    return (xp, xs) + stacked
```

```python
import functools

import jax
import jax.numpy as jnp
from jax import lax
from jax.experimental import pallas as pl
from jax.experimental.pallas import tpu as pltpu

HEAD_DIM = 64
CONV_WIDTH = 3
PAGE_SIZE = 128
EPS = 1e-6
LOG2E = 1.4426950408889634
LANES = 128
MXU_DIM = 256
VMEM_LIMIT = 56 * 1024 * 1024
ATTN_HEADS_PER_STEP = 8
DECODE_PAGES_PER_CHUNK = 8
DECODE_RING_CHUNKS = 3

F32 = jnp.float32
BF16 = jnp.bfloat16


def _log_sigmoid(x):
    return jnp.minimum(x, 0.0) - jnp.log1p(jnp.exp(-jnp.abs(x)))


def _rms_scale(x):
    return lax.rsqrt(jnp.mean(x * x, axis=-1, keepdims=True) + EPS)


def _head_mean_sq(y, seg):
    y2 = (y * y).astype(BF16)
    parts = [jnp.dot(y2[:, c:c + MXU_DIM], seg, preferred_element_type=F32)
             for c in range(0, y.shape[1], MXU_DIM)]
    return parts[0] if len(parts) == 1 else jnp.concatenate(parts, axis=1)


def _head_norm(y, gain, seg):
    return y * lax.rsqrt(_head_mean_sq(y, seg) + EPS) * gain


def _dotT(a, b):
    return lax.dot_general(a, b, (((1,), (1,)), ((), ())), preferred_element_type=F32)


def _const_spec(shape):
    nd = len(shape)
    return pl.BlockSpec(shape, lambda *_: (0,) * nd, pipeline_mode=pl.Buffered(1))


def _split3(x):
    hi = x.astype(BF16)
    r1 = x - hi.astype(F32)
    mid = r1.astype(BF16)
    lo = (r1 - mid.astype(F32)).astype(BF16)
    return hi, mid, lo


def _store_head_major(y, extra, o_ref):
    low = lax.broadcasted_iota(jnp.int32, (1, LANES), 1) < HEAD_DIM
    for h in range(o_ref.shape[1]):
        pair = y[:, (h // 2) * LANES:(h // 2 + 1) * LANES]
        if h % 2:
            pair = pltpu.roll(pair, HEAD_DIM, 1)
        o_ref[0, h] = jnp.where(low, pair, extra(h)).astype(BF16)


def _inproj_prompt_kernel(x_ref, g_ref, w_ref, wft_ref, bf_ref, cw_ref, qg_ref, kg_ref, og_ref, seg_ref, eye_ref,
                          ycn_ref, qa_ref, ka_ref, va_ref, kout_ref, vout_ref, lft_ref, cs_ref,
                          ucar_ref, ccar_ref, *, dc, da):
    j = pl.program_id(1)
    tm = x_ref.shape[1]
    nh = lft_ref.shape[1]

    @pl.when(j == 0)
    def _():
        ucar_ref[...] = jnp.zeros_like(ucar_ref)
        ccar_ref[...] = jnp.zeros_like(ccar_ref)

    x = x_ref[0]
    xn = (x * _rms_scale(x) * g_ref[...]).astype(BF16)
    seg = seg_ref[...]

    def proj(c0, n):
        return jnp.dot(xn, w_ref[:, c0:c0 + n], preferred_element_type=F32)

    lf = _log_sigmoid(_dotT(wft_ref[...], xn)[:nh] + bf_ref[...])
    lft_ref[0] = lf
    lane_t = lax.broadcasted_iota(jnp.int32, lf.shape, 1)
    c = lf
    sh = 1
    while sh < tm:
        c = c + jnp.where(lane_t >= sh, pltpu.roll(c, sh, 1), 0.0)
        sh *= 2
    c = c + ccar_ref[:, 0:1]
    ccar_ref[...] = jnp.broadcast_to(c[:, tm - 1:tm], ccar_ref.shape)
    parts = _split3(c * (-LOG2E))
    c3 = jnp.concatenate([jnp.zeros((HEAD_DIM, tm), BF16), *parts,
                          jnp.zeros((LANES - HEAD_DIM - 3 * nh, tm), BF16)], axis=0)
    bias_cols = _dotT(eye_ref[...], c3)

    ub = proj(0, dc)
    u = proj(dc, dc) * proj(2 * dc, dc)
    prev = ucar_ref[...]
    row = lax.broadcasted_iota(jnp.int32, (tm, 1), 0)
    u1 = jnp.where(row == 0, prev[7:8], pltpu.roll(u, 1, 0))
    u2 = jnp.where(row == 0, prev[6:7], jnp.where(row == 1, prev[7:8], pltpu.roll(u, 2, 0)))
    yc = ub * (cw_ref[0:1] * u2 + cw_ref[1:2] * u1 + cw_ref[2:3] * u)
    ycn_ref[0] = _head_norm(yc, og_ref[...], seg).astype(BF16)
    ucar_ref[...] = u[tm - 8:tm]
    cs_ref[0] = u[tm - (CONV_WIDTH - 1):tm]

    lane = lax.broadcasted_iota(jnp.int32, (1, LANES), 1)
    q = _head_norm(proj(3 * dc, da), qg_ref[...], seg) * (HEAD_DIM ** -0.5 * LOG2E)
    _store_head_major(
        q, lambda h: jnp.where((lane >= HEAD_DIM) & (lane < HEAD_DIM + 3 * nh) & (lane % nh == h), 1.0, 0.0), qa_ref)
    k = _head_norm(proj(3 * dc + da, da), kg_ref[...], seg)
    kout_ref[0] = k
    _store_head_major(k, lambda h: bias_cols, ka_ref)
    v = proj(3 * dc + 2 * da, da)
    vout_ref[0] = v
    _store_head_major(v, lambda h: jnp.where(lane == HEAD_DIM, 1.0, 0.0), va_ref)


def _inproj_prompt(x, g, w, wft, bfc, cw, qg, kg, og, seg, eye, *, tm):
    b, s, d = x.shape
    dc = cw.shape[1]
    da = qg.shape[1]
    nh = bfc.shape[0]
    grid = (b, s // tm)
    row_blk = lambda n: pl.BlockSpec((1, tm, n), lambda i, j: (i, j, 0))
    head_blk = pl.BlockSpec((1, nh, tm, LANES), lambda i, j: (i, 0, j, 0))
    out_shape = (
        jax.ShapeDtypeStruct((b, s, dc), BF16),
        jax.ShapeDtypeStruct((b, nh, s, LANES), BF16),
        jax.ShapeDtypeStruct((b, nh, s, LANES), BF16),
        jax.ShapeDtypeStruct((b, nh, s, LANES), BF16),
        jax.ShapeDtypeStruct((b, s, da), F32),
        jax.ShapeDtypeStruct((b, s, da), F32),
        jax.ShapeDtypeStruct((b, nh, s), F32),
        jax.ShapeDtypeStruct((b, CONV_WIDTH - 1, dc), F32),
    )
    out_specs = (
        row_blk(dc), head_blk, head_blk, head_blk, row_blk(da), row_blk(da),
        pl.BlockSpec((1, nh, tm), lambda i, j: (i, 0, j)),
        pl.BlockSpec((1, CONV_WIDTH - 1, dc), lambda i, j: (i, 0, 0)),
    )
    in_specs = [
        row_blk(d), _const_spec(g.shape), _const_spec(w.shape), _const_spec(wft.shape),
        _const_spec(bfc.shape), _const_spec(cw.shape), _const_spec(qg.shape), _const_spec(kg.shape),
        _const_spec(og.shape), _const_spec(seg.shape), _const_spec(eye.shape),
    ]
    return pl.pallas_call(
        functools.partial(_inproj_prompt_kernel, dc=dc, da=da),
        grid=grid, in_specs=in_specs, out_specs=out_specs, out_shape=out_shape,
        scratch_shapes=[pltpu.VMEM((8, dc), F32), pltpu.VMEM((nh, LANES), F32)],
        compiler_params=pltpu.CompilerParams(
            dimension_semantics=("arbitrary", "arbitrary"), vmem_limit_bytes=VMEM_LIMIT),
        name="inproj_prompt",
    )(x, g, w, wft, bfc, cw, qg, kg, og, seg, eye)


def _attn_prompt_kernel(q_ref, k_ref, v_ref, og_ref, seg_ref, o_ref, m_ref, acc_ref, *, tk):
    qi = pl.program_id(2)
    nhs, tq = q_ref.shape[1:3]
    rep = tk // LANES
    m_ref[...] = jnp.full_like(m_ref, -jnp.inf)
    acc_ref[...] = jnp.zeros_like(acc_ref)

    def tile(ki, masked):
        k0 = pl.multiple_of(ki * tk, tk)
        if masked:
            keep = (lax.broadcasted_iota(jnp.int32, (tq, tk), 1)
                    <= lax.broadcasted_iota(jnp.int32, (tq, tk), 0))
        for hh in range(nhs):
            s = _dotT(q_ref[0, hh], k_ref[0, hh, pl.ds(k0, tk), :])
            if masked:
                s = jnp.where(keep, s, -jnp.inf)
            m_old = m_ref[hh]
            m_new = jnp.maximum(m_old, jnp.max(s, axis=1, keepdims=True))
            p = jnp.exp2(s - jnp.tile(m_new, (1, rep)))
            alpha = jnp.exp2(m_old - m_new)
            acc_ref[hh] = alpha * acc_ref[hh] + jnp.dot(
                p.astype(BF16), v_ref[0, hh, pl.ds(k0, tk), :], preferred_element_type=F32)
            m_ref[hh] = m_new

    def body(ki, carry):
        tile(ki, False)
        return carry

    lax.fori_loop(0, qi, body, 0)
    tile(qi, True)
    low = lax.broadcasted_iota(jnp.int32, (1, LANES), 1) < HEAD_DIM
    seg = seg_ref[...]
    for hp in range(nhs // 2):
        outs = []
        for hh in (2 * hp, 2 * hp + 1):
            a = acc_ref[hh]
            outs.append(a / a[:, HEAD_DIM:HEAD_DIM + 1])
        o = jnp.where(low, outs[0], pltpu.roll(outs[1], HEAD_DIM, 1))
        ms = jnp.dot((o * o).astype(BF16), seg, preferred_element_type=F32)
        o_ref[0, :, hp * LANES:(hp + 1) * LANES] = (
            o * lax.rsqrt(ms + EPS) * og_ref[:, hp * LANES:(hp + 1) * LANES]).astype(BF16)


def _attn_prompt(qa, ka, va, og, seg, *, tq, nhs):
    b, nh, s, _ = qa.shape
    grid = (b, nh // nhs, s // tq)
    wo = nhs * HEAD_DIM
    in_specs = [
        pl.BlockSpec((1, nhs, tq, LANES), lambda i, h, j: (i, h, j, 0)),
        pl.BlockSpec((1, nhs, s, LANES), lambda i, h, j: (i, h, 0, 0)),
        pl.BlockSpec((1, nhs, s, LANES), lambda i, h, j: (i, h, 0, 0)),
        pl.BlockSpec((1, wo), lambda i, h, j: (0, h)),
        _const_spec(seg.shape),
    ]
    return pl.pallas_call(
        functools.partial(_attn_prompt_kernel, tk=tq),
        grid=grid, in_specs=in_specs,
        out_specs=pl.BlockSpec((1, tq, wo), lambda i, h, j: (i, j, h)),
        out_shape=jax.ShapeDtypeStruct((b, s, nh * HEAD_DIM), BF16),
        scratch_shapes=[pltpu.VMEM((nhs, tq, LANES), F32), pltpu.VMEM((nhs, tq, LANES), F32)],
        compiler_params=pltpu.CompilerParams(
            dimension_semantics=("arbitrary", "arbitrary", "arbitrary"), vmem_limit_bytes=VMEM_LIMIT),
        name="attn_prompt",
    )(qa, ka, va, og, seg)


def _mix_ffn_kernel(x_ref, yc_ref, ya_ref, wo_ref, fg_ref, wg_ref, wu_ref, wd_ref, o_ref, *, ff_chunk):
    dc = yc_ref.shape[1]
    mix = jnp.dot(yc_ref[...], wo_ref[0:dc, :], preferred_element_type=F32)
    mix = mix + jnp.dot(ya_ref[...], wo_ref[dc:, :], preferred_element_type=F32)
    h = x_ref[...] + mix
    hn = (h * _rms_scale(h) * fg_ref[...]).astype(BF16)
    dff = wg_ref.shape[1]
    acc = None
    for c0 in range(0, dff, ff_chunk):
        gate = jnp.dot(hn, wg_ref[:, c0:c0 + ff_chunk], preferred_element_type=F32)
        up = jnp.dot(hn, wu_ref[:, c0:c0 + ff_chunk], preferred_element_type=F32)
        a = (gate * jax.nn.sigmoid(gate) * up).astype(BF16)
        part = jnp.dot(a, wd_ref[c0:c0 + ff_chunk, :], preferred_element_type=F32)
        acc = part if acc is None else acc + part
    o_ref[...] = h + acc


def _mix_ffn(x, yc, ya, wo, fg, wg, wu, wd, *, tm, ff_chunk):
    r, d = x.shape
    row_blk = lambda n: pl.BlockSpec((tm, n), lambda i: (i, 0))
    in_specs = [row_blk(d), row_blk(yc.shape[1]), row_blk(ya.shape[1]),
                _const_spec(wo.shape), _const_spec(fg.shape), _const_spec(wg.shape),
                _const_spec(wu.shape), _const_spec(wd.shape)]
    return pl.pallas_call(
        functools.partial(_mix_ffn_kernel, ff_chunk=ff_chunk),
        grid=(r // tm,), in_specs=in_specs, out_specs=row_blk(d),
        out_shape=jax.ShapeDtypeStruct((r, d), F32),
        compiler_params=pltpu.CompilerParams(
            dimension_semantics=("arbitrary",), vmem_limit_bytes=VMEM_LIMIT),
        name="mix_ffn",
    )(x, yc, ya, wo, fg, wg, wu, wd)


def _inproj_sample_kernel(x_ref, g_ref, w_ref, wft_ref, bf_ref, cw_ref, qg_ref, kg_ref, og_ref, seg_ref,
                          st_ref, ycn_ref, q_ref, k_ref, v_ref, lft_ref, gt_ref, cs_ref, *, dc, da, nt):
    nb = x_ref.shape[0] // nt
    x = x_ref[...]
    xn = (x * _rms_scale(x) * g_ref[...]).astype(BF16)
    seg = seg_ref[...]

    def proj(c0, n):
        return jnp.dot(xn, w_ref[:, c0:c0 + n], preferred_element_type=F32)

    ub = proj(0, dc)
    u = proj(dc, dc) * proj(2 * dc, dc)
    ucat = [st_ref[i] for i in range(CONV_WIDTH - 1)] + [u[t * nb:(t + 1) * nb] for t in range(nt)]
    conv = jnp.concatenate(
        [sum(cw_ref[jj:jj + 1] * ucat[t + jj] for jj in range(CONV_WIDTH)) for t in range(nt)], axis=0)
    ycn_ref[...] = _head_norm(ub * conv, og_ref[...], seg).astype(BF16)
    for i in range(CONV_WIDTH - 1):
        cs_ref[i] = ucat[nt + i]

    q_ref[...] = _head_norm(proj(3 * dc, da), qg_ref[...], seg) * (HEAD_DIM ** -0.5)
    k_ref[...] = _head_norm(proj(3 * dc + da, da), kg_ref[...], seg)
    v_ref[...] = proj(3 * dc + 2 * da, da)

    nh = lft_ref.shape[0]
    lf = _log_sigmoid(_dotT(wft_ref[...], xn)[:nh] + bf_ref[...])
    lft_ref[...] = lf
    run = lf[:, 0:nb]
    gs = [run]
    for t in range(1, nt):
        run = run + lf[:, t * nb:(t + 1) * nb]
        gs.append(run)
    gt_ref[...] = jnp.concatenate(gs, axis=1)


def _inproj_sample(x, g, w, wft, bfc, cw, qg, kg, og, seg, st, *, nt):
    r, d = x.shape
    dc = cw.shape[1]
    da = qg.shape[1]
    nh = bfc.shape[0]
    args = (x, g, w, wft, bfc, cw, qg, kg, og, seg, st)
    out_shape = (
        jax.ShapeDtypeStruct((r, dc), BF16),
        jax.ShapeDtypeStruct((r, da), F32),
        jax.ShapeDtypeStruct((r, da), F32),
        jax.ShapeDtypeStruct((r, da), F32),
        jax.ShapeDtypeStruct((nh, r), F32),
        jax.ShapeDtypeStruct((nh, r), F32),
        jax.ShapeDtypeStruct(st.shape, F32),
    )
    full = lambda a: pl.BlockSpec(a.shape, lambda i: (0,) * len(a.shape))
    return pl.pallas_call(
        functools.partial(_inproj_sample_kernel, dc=dc, da=da, nt=nt),
        grid=(1,), in_specs=[full(a) for a in args], out_specs=tuple(full(o) for o in out_shape),
        out_shape=out_shape,
        compiler_params=pltpu.CompilerParams(
            dimension_semantics=("arbitrary",), vmem_limit_bytes=VMEM_LIMIT),
        name="inproj_sample",
    )(*args)


def _ffn_decode_kernel(pt_ref, x_ref, yc_ref, ya_ref, wo_ref, fg_ref, wg_ref, wu_ref, wd_ref,
                       qbd_ref, kn_ref, vn_ref, gneg_ref, og_ref, seg_ref, ck_hbm, cv_hbm, clf_hbm,
                       o_ref, oa_ref, kbuf, vbuf, lbuf, ksem, vsem, lsem, *, ch, ring, nt, nh, ff_chunk):
    i = pl.program_id(0)
    nb, npages = pt_ref.shape
    nch = npages // ch

    def page_copies(b, c):
        out = []
        for j in range(ch):
            page = pt_ref[b, npages - 1 - (c * ch + j)]
            s = (c % ring) * ch + j
            out.append(pltpu.make_async_copy(ck_hbm.at[page], kbuf.at[s], ksem.at[s]))
            out.append(pltpu.make_async_copy(cv_hbm.at[page], vbuf.at[s], vsem.at[s]))
            out.append(pltpu.make_async_copy(clf_hbm.at[page], lbuf.at[s], lsem.at[s]))
        return out

    def start_chunk(b, c):
        for cp in page_copies(b, c):
            cp.start()

    def wait_chunk(b, c):
        for cp in page_copies(b, c):
            cp.wait()

    @pl.when(i == 0)
    def _():
        for c in range(min(ring, nch)):
            start_chunk(0, c)

    qbd = qbd_ref[0]
    sn = _dotT(qbd, kn_ref[0]) + gneg_ref[0]
    r_i = lax.broadcasted_iota(jnp.int32, sn.shape, 0)
    c_i = lax.broadcasted_iota(jnp.int32, sn.shape, 1)
    sn = jnp.where(c_i * nh <= r_i, sn, -jnp.inf)
    m = jnp.max(sn, axis=1, keepdims=True)
    p0 = jnp.exp(sn - m)
    l = jnp.sum(p0, axis=1, keepdims=True)
    acc = jnp.dot(p0.astype(BF16), vn_ref[0], preferred_element_type=F32)
    run = jnp.zeros((nh, 1), F32)
    lane = lax.broadcasted_iota(jnp.int32, (nh, PAGE_SIZE), 1)

    def decode_scores(c, run):
        slots = [(c % ring) * ch + j for j in range(ch)]
        decays = []
        for s_ in slots:
            lf = lbuf[s_]
            suf = lf
            sh = 1
            while sh < PAGE_SIZE:
                suf = suf + jnp.where(lane < PAGE_SIZE - sh, pltpu.roll(suf, PAGE_SIZE - sh, 1), 0.0)
                sh *= 2
            decays.append(jnp.concatenate([run + (suf - lf)] * nt, axis=0))
            run = run + suf[:, 0:1]
        kt = jnp.concatenate([kbuf[s_].astype(BF16) for s_ in slots], axis=1)
        return jnp.dot(qbd, kt, preferred_element_type=F32) + jnp.concatenate(decays, axis=1), run

    def decode_values(c, s, m, l, acc):
        slots = [(c % ring) * ch + j for j in range(ch)]
        m_new = jnp.maximum(m, jnp.max(s, axis=1, keepdims=True))
        alpha = jnp.exp(m - m_new)
        p = jnp.exp(s - m_new)
        l = alpha * l + jnp.sum(p, axis=1, keepdims=True)
        vt = jnp.concatenate([vbuf[s_].astype(BF16) for s_ in slots], axis=1)
        return m_new, l, alpha * acc + _dotT(p.astype(BF16), vt)

    def decode_finish(l, acc):
        o = acc / l
        r_o = lax.broadcasted_iota(jnp.int32, o.shape, 0)
        c_o = lax.broadcasted_iota(jnp.int32, o.shape, 1)
        o = jnp.where(c_o // HEAD_DIM == r_o % nh, o, 0.0)
        y = jnp.concatenate([jnp.sum(o[t * nh:(t + 1) * nh], axis=0, keepdims=True) for t in range(nt)], axis=0)
        oa_ref[0] = _head_norm(y, og_ref[...], seg_ref[...])

    dc = yc_ref.shape[1]
    dff = wg_ref.shape[1]
    ffn = {}

    def ffn_first(ph):
        if ph == 0:
            mix = jnp.dot(yc_ref[...], wo_ref[0:dc, :], preferred_element_type=F32)
            mix = mix + jnp.dot(ya_ref[...], wo_ref[dc:, :], preferred_element_type=F32)
            ffn["h"] = x_ref[...] + mix
            return
        c0 = (ph - 1) * ff_chunk
        hn = ffn["hn"]
        ffn["gate"] = jnp.dot(hn, wg_ref[:, c0:c0 + ff_chunk], preferred_element_type=F32)
        ffn["up"] = jnp.dot(hn, wu_ref[:, c0:c0 + ff_chunk], preferred_element_type=F32)

    def ffn_second(ph):
        if ph == 0:
            h = ffn["h"]
            ffn["hn"] = (h * _rms_scale(h) * fg_ref[...]).astype(BF16)
            ffn["acc"] = None
            return
        c0 = (ph - 1) * ff_chunk
        gate = ffn["gate"]
        a = (gate * jax.nn.sigmoid(gate) * ffn["up"]).astype(BF16)
        part = jnp.dot(a, wd_ref[c0:c0 + ff_chunk, :], preferred_element_type=F32)
        ffn["acc"] = part if ffn["acc"] is None else ffn["acc"] + part

    n_ffn = 1 + dff // ff_chunk
    for ph in range(max(n_ffn, nch + 1)):
        if ph < nch:
            wait_chunk(i, ph)
        if ph < n_ffn:
            ffn_first(ph)
        if ph < nch:
            s_ph, run = decode_scores(ph, run)
        if ph < n_ffn:
            ffn_second(ph)
        if ph < nch:
            m, l, acc = decode_values(ph, s_ph, m, l, acc)
        if ph == nch:
            @pl.when(i + 1 < nb)
            def _():
                for c in range(min(ring, nch)):
                    start_chunk(i + 1, c)
            decode_finish(l, acc)
        if ph + ring < nch:
            start_chunk(i, ph + ring)
    o_ref[...] = ffn["h"] + ffn["acc"]


def _mix_ffn_decode(page_table, x, yc, ya, wo, fg, wg, wu, wd, qbd, kn, vn, gneg, og, seg, ck, cv, clf,
                    *, ch, ring, nt, nh, ff_chunk):
    r, d = x.shape
    nb, npages = page_table.shape
    da = qbd.shape[2]
    assert r % nb == 0 and npages % ch == 0 and wg.shape[1] % ff_chunk == 0
    tm = r // nb
    row_blk = lambda n: pl.BlockSpec((tm, n), lambda i, pt: (i, 0))
    const = lambda a: pl.BlockSpec(a.shape, lambda i, pt: (0,) * a.ndim, pipeline_mode=pl.Buffered(1))
    per_b = lambda a: pl.BlockSpec((1,) + a.shape[1:], lambda i, pt: (i,) + (0,) * (a.ndim - 1))
    hbm = pl.BlockSpec(memory_space=pl.ANY)
    in_specs = [row_blk(d), row_blk(yc.shape[1]), row_blk(ya.shape[1]),
                const(wo), const(fg), const(wg), const(wu), const(wd),
                per_b(qbd), per_b(kn), per_b(vn), per_b(gneg), const(og), const(seg), hbm, hbm, hbm]
    nslot = ring * ch
    grid_spec = pltpu.PrefetchScalarGridSpec(
        num_scalar_prefetch=1, grid=(nb,), in_specs=in_specs,
        out_specs=(row_blk(d), pl.BlockSpec((1, nt, da), lambda i, pt: (i, 0, 0))),
        scratch_shapes=[pltpu.VMEM((nslot, da, PAGE_SIZE), F32), pltpu.VMEM((nslot, da, PAGE_SIZE), F32),
                        pltpu.VMEM((nslot, nh, PAGE_SIZE), F32),
                        pltpu.SemaphoreType.DMA((nslot,)), pltpu.SemaphoreType.DMA((nslot,)),
                        pltpu.SemaphoreType.DMA((nslot,))])
    return pl.pallas_call(
        functools.partial(_ffn_decode_kernel, ch=ch, ring=ring, nt=nt, nh=nh, ff_chunk=ff_chunk),
        grid_spec=grid_spec,
        out_shape=(jax.ShapeDtypeStruct((r, d), F32), jax.ShapeDtypeStruct((nb, nt, da), F32)),
        compiler_params=pltpu.CompilerParams(
            dimension_semantics=("arbitrary",), vmem_limit_bytes=VMEM_LIMIT),
        name="mix_ffn_decode",
    )(page_table, x, yc, ya, wo, fg, wg, wu, wd, qbd, kn, vn, gneg, og, seg, ck, cv, clf)


def _pick(n, pref):
    t = min(n, pref)
    while n % t:
        t //= 2
    return t


def _layer(xp, xs, cache_k, cache_v, cache_logf, state_conv, page_table,
           norm_g, w_in, b_f, conv_w, q_g, k_g, out_g, w_out, ffn_g, w_gate, w_up, w_down):
    bsz, seq, d = xp.shape
    nb, nt, _ = xs.shape
    dc = conv_w.shape[1]
    nh = b_f.shape[0]
    da = nh * HEAD_DIM
    npages = page_table.shape[1]

    g = norm_g.reshape(1, d)
    w_main = w_in[:, :3 * dc + 3 * da].astype(BF16)
    wft = jnp.zeros((16, d), BF16).at[:nh].set(w_in[:, 3 * dc + 3 * da:].T.astype(BF16))
    bfc = b_f.reshape(nh, 1)
    qg = jnp.tile(q_g, nh).reshape(1, da)
    kg = jnp.tile(k_g, nh).reshape(1, da)
    og_c = out_g[:dc].reshape(1, dc)
    og_a = out_g[dc:].reshape(1, da)
    gi = jnp.arange(MXU_DIM) // HEAD_DIM
    seg = jnp.where(gi[:, None] == gi[None, :], 1.0 / HEAD_DIM, 0.0).astype(BF16)
    wo = w_out.astype(BF16)
    fg = ffn_g.reshape(1, d)
    wg = w_gate.astype(BF16)
    wu = w_up.astype(BF16)
    wd = w_down.astype(BF16)
    dff = wg.shape[1]

    tm = _pick(seq, 512)
    eye = jnp.eye(tm, dtype=BF16)
    ycn, qa, ka, va, k_out, v_out, lft, cs_p = _inproj_prompt(
        xp, g, w_main, wft, bfc, conv_w, qg, kg, og_c, seg, eye, tm=tm)
    yan = _attn_prompt(qa, ka, va, og_a, seg[:LANES, :LANES], tq=_pick(seq, 512), nhs=ATTN_HEADS_PER_STEP)
    rows = bsz * seq
    nk_p = k_out.reshape(bsz, seq, nh, HEAD_DIM)
    nv_p = v_out.reshape(bsz, seq, nh, HEAD_DIM)
    nl_p = jnp.transpose(lft, (0, 2, 1))

    rs = nb * nt
    xs_tm = jnp.transpose(xs, (1, 0, 2)).reshape(rs, d)
    st = jnp.transpose(state_conv, (1, 0, 2))
    ycn_s, q_s, k_s, v_s, lft_s, gt_s, cs_s = _inproj_sample(
        xs_tm, g, w_main, wft, bfc, conv_w, qg, kg, og_c, seg, st, nt=nt)

    to_bt = lambda a: jnp.transpose(a.reshape(nt, nb, -1), (1, 0, 2))
    eye_h = jnp.eye(nh, dtype=F32)
    qbd = jnp.einsum('tbhd,hg->bthgd', q_s.reshape(nt, nb, nh, HEAD_DIM), eye_h)
    qbd = qbd.reshape(nb, nt * nh, da).astype(BF16)
    pad_rows = 16 - nt
    kn = jnp.pad(to_bt(k_s), ((0, 0), (0, pad_rows), (0, 0))).astype(BF16)
    vn = jnp.pad(to_bt(v_s), ((0, 0), (0, pad_rows), (0, 0))).astype(BF16)
    gneg = -jnp.transpose(gt_s.reshape(nh, nt, nb), (2, 0, 1))
    gneg = jnp.pad(gneg, ((0, 0), (0, 0), (0, 16 - nt)))
    gneg = jnp.tile(gneg, (1, nt, 1))
    nphys = cache_k.shape[0]
    ck = jnp.transpose(cache_k, (0, 2, 3, 1)).reshape(nphys, da, PAGE_SIZE)
    cv = jnp.transpose(cache_v, (0, 2, 3, 1)).reshape(nphys, da, PAGE_SIZE)
    clf = jnp.transpose(cache_logf, (0, 2, 1))
    yp, yan_s = _mix_ffn_decode(
        page_table, xp.reshape(rows, d), ycn.reshape(rows, dc), yan.reshape(rows, da), wo, fg, wg, wu, wd,
        qbd, kn, vn, gneg, og_a, seg, ck, cv, clf,
        ch=_pick(npages, DECODE_PAGES_PER_CHUNK), ring=DECODE_RING_CHUNKS, nt=nt, nh=nh, ff_chunk=MXU_DIM)
    y_prompt = yp.reshape(bsz, seq, d)
    ys = _mix_ffn(xs.reshape(rs, d), to_bt(ycn_s).reshape(rs, dc), yan_s.reshape(rs, da).astype(BF16),
                  wo, fg, wg, wu, wd, tm=_pick(rs, 512), ff_chunk=dff)
    y_sample = ys.reshape(nb, nt, d)
    nk_s = to_bt(k_s).reshape(nb, nt, nh, HEAD_DIM)
    nv_s = to_bt(v_s).reshape(nb, nt, nh, HEAD_DIM)
    nl_s = jnp.transpose(lft_s.reshape(nh, nt, nb), (2, 1, 0))
    cs_s = jnp.transpose(cs_s, (1, 0, 2))
    return (y_prompt, y_sample, nk_p, nv_p, nl_p, cs_p, nk_s, nv_s, nl_s, cs_s)


def kernel(x_prompt, x_sample, cache_k, cache_v, cache_logf, state_conv, page_table,
           norm_mix_g, w_in, b_f, conv_w, q_norm_g, k_norm_g, out_norm_g, w_out,
           norm_ffn_g, w_gate, w_up, w_down):
    depth = w_in.shape[0]
    xp, xs = x_prompt, x_sample
    outs = []
    for l in range(depth):
        res = _layer(xp, xs, cache_k[l], cache_v[l], cache_logf[l], state_conv[l], page_table,
                     norm_mix_g[l], w_in[l], b_f[l], conv_w[l], q_norm_g[l], k_norm_g[l], out_norm_g[l],
                     w_out[l], norm_ffn_g[l], w_gate[l], w_up[l], w_down[l])
        xp, xs = res[0], res[1]
        outs.append(res[2:])
    stacked = tuple(jnp.stack([o[i] for o in outs]) for i in range(8))
    return (xp, xs) + stacked
```

```python
import functools

import jax
import jax.numpy as jnp
from jax import lax
from jax.experimental import pallas as pl
from jax.experimental.pallas import tpu as pltpu

HEAD_DIM = 64
CONV_WIDTH = 3
PAGE_SIZE = 128
EPS = 1e-6
LOG2E = 1.4426950408889634
LANES = 128
MXU_DIM = 256
VMEM_LIMIT = 56 * 1024 * 1024
ATTN_HEADS_PER_STEP = 8
DECODE_PAGES_PER_CHUNK = 8
DECODE_RING_CHUNKS = 3

F32 = jnp.float32
BF16 = jnp.bfloat16


def _log_sigmoid(x):
    return jnp.minimum(x, 0.0) - jnp.log1p(jnp.exp(-jnp.abs(x)))


def _rms_scale(x):
    return lax.rsqrt(jnp.mean(x * x, axis=-1, keepdims=True) + EPS)


def _head_mean_sq(y, seg):
    y2 = (y * y).astype(BF16)
    parts = [jnp.dot(y2[:, c:c + MXU_DIM], seg, preferred_element_type=F32)
             for c in range(0, y.shape[1], MXU_DIM)]
    return parts[0] if len(parts) == 1 else jnp.concatenate(parts, axis=1)


def _head_norm(y, gain, seg):
    return y * lax.rsqrt(_head_mean_sq(y, seg) + EPS) * gain


def _dotT(a, b):
    return lax.dot_general(a, b, (((1,), (1,)), ((), ())), preferred_element_type=F32)


def _const_spec(shape):
    nd = len(shape)
    return pl.BlockSpec(shape, lambda *_: (0,) * nd, pipeline_mode=pl.Buffered(1))


def _split3(x):
    hi = x.astype(BF16)
    r1 = x - hi.astype(F32)
    mid = r1.astype(BF16)
    lo = (r1 - mid.astype(F32)).astype(BF16)
    return hi, mid, lo


def _store_head_major(y, extra, o_ref):
    low = lax.broadcasted_iota(jnp.int32, (1, LANES), 1) < HEAD_DIM
    for h in range(o_ref.shape[1]):
        pair = y[:, (h // 2) * LANES:(h // 2 + 1) * LANES]
        if h % 2:
            pair = pltpu.roll(pair, HEAD_DIM, 1)
        o_ref[0, h] = jnp.where(low, pair, extra(h)).astype(BF16)


def _inproj_prompt_kernel(x_ref, g_ref, w_ref, wft_ref, bf_ref, cw_ref, qg_ref, kg_ref, og_ref, seg_ref, eye_ref,
                          ycn_ref, qa_ref, ka_ref, va_ref, kout_ref, vout_ref, lft_ref, cs_ref,
                          ucar_ref, ccar_ref, *, dc, da):
    j = pl.program_id(1)
    tm = x_ref.shape[1]
    nh = lft_ref.shape[1]

    @pl.when(j == 0)
    def _():
        ucar_ref[...] = jnp.zeros_like(ucar_ref)
        ccar_ref[...] = jnp.zeros_like(ccar_ref)

    x = x_ref[0]
    xn = (x * _rms_scale(x) * g_ref[...]).astype(BF16)
    seg = seg_ref[...]

    def proj(c0, n):
        return jnp.dot(xn, w_ref[:, c0:c0 + n], preferred_element_type=F32)

    lf = _log_sigmoid(_dotT(wft_ref[...], xn)[:nh] + bf_ref[...])
    lft_ref[0] = lf
    lane_t = lax.broadcasted_iota(jnp.int32, lf.shape, 1)
    c = lf
    sh = 1
    while sh < tm:
        c = c + jnp.where(lane_t >= sh, pltpu.roll(c, sh, 1), 0.0)
        sh *= 2
    c = c + ccar_ref[:, 0:1]
    ccar_ref[...] = jnp.broadcast_to(c[:, tm - 1:tm], ccar_ref.shape)
    parts = _split3(c * (-LOG2E))
    c3 = jnp.concatenate([jnp.zeros((HEAD_DIM, tm), BF16), *parts,
                          jnp.zeros((LANES - HEAD_DIM - 3 * nh, tm), BF16)], axis=0)
    bias_cols = _dotT(eye_ref[...], c3)

    ub = proj(0, dc)
    u = proj(dc, dc) * proj(2 * dc, dc)
    prev = ucar_ref[...]
    row = lax.broadcasted_iota(jnp.int32, (tm, 1), 0)
    u1 = jnp.where(row == 0, prev[7:8], pltpu.roll(u, 1, 0))
    u2 = jnp.where(row == 0, prev[6:7], jnp.where(row == 1, prev[7:8], pltpu.roll(u, 2, 0)))
    yc = ub * (cw_ref[0:1] * u2 + cw_ref[1:2] * u1 + cw_ref[2:3] * u)
    ycn_ref[0] = _head_norm(yc, og_ref[...], seg).astype(BF16)
    ucar_ref[...] = u[tm - 8:tm]
    cs_ref[0] = u[tm - (CONV_WIDTH - 1):tm]

    lane = lax.broadcasted_iota(jnp.int32, (1, LANES), 1)
    q = _head_norm(proj(3 * dc, da), qg_ref[...], seg) * (HEAD_DIM ** -0.5 * LOG2E)
    _store_head_major(
        q, lambda h: jnp.where((lane >= HEAD_DIM) & (lane < HEAD_DIM + 3 * nh) & (lane % nh == h), 1.0, 0.0), qa_ref)
    k = _head_norm(proj(3 * dc + da, da), kg_ref[...], seg)
    kout_ref[0] = k
    _store_head_major(k, lambda h: bias_cols, ka_ref)
    v = proj(3 * dc + 2 * da, da)
    vout_ref[0] = v
    _store_head_major(v, lambda h: jnp.where(lane == HEAD_DIM, 1.0, 0.0), va_ref)


def _inproj_prompt(x, g, w, wft, bfc, cw, qg, kg, og, seg, eye, *, tm):
    b, s, d = x.shape
    dc = cw.shape[1]
    da = qg.shape[1]
    nh = bfc.shape[0]
    grid = (b, s // tm)
    row_blk = lambda n: pl.BlockSpec((1, tm, n), lambda i, j: (i, j, 0))
    head_blk = pl.BlockSpec((1, nh, tm, LANES), lambda i, j: (i, 0, j, 0))
    out_shape = (
        jax.ShapeDtypeStruct((b, s, dc), BF16),
        jax.ShapeDtypeStruct((b, nh, s, LANES), BF16),
        jax.ShapeDtypeStruct((b, nh, s, LANES), BF16),
        jax.ShapeDtypeStruct((b, nh, s, LANES), BF16),
        jax.ShapeDtypeStruct((b, s, da), F32),
        jax.ShapeDtypeStruct((b, s, da), F32),
        jax.ShapeDtypeStruct((b, nh, s), F32),
        jax.ShapeDtypeStruct((b, CONV_WIDTH - 1, dc), F32),
    )
    out_specs = (
        row_blk(dc), head_blk, head_blk, head_blk, row_blk(da), row_blk(da),
        pl.BlockSpec((1, nh, tm), lambda i, j: (i, 0, j)),
        pl.BlockSpec((1, CONV_WIDTH - 1, dc), lambda i, j: (i, 0, 0)),
    )
    in_specs = [
        row_blk(d), _const_spec(g.shape), _const_spec(w.shape), _const_spec(wft.shape),
        _const_spec(bfc.shape), _const_spec(cw.shape), _const_spec(qg.shape), _const_spec(kg.shape),
        _const_spec(og.shape), _const_spec(seg.shape), _const_spec(eye.shape),
    ]
    return pl.pallas_call(
        functools.partial(_inproj_prompt_kernel, dc=dc, da=da),
        grid=grid, in_specs=in_specs, out_specs=out_specs, out_shape=out_shape,
        scratch_shapes=[pltpu.VMEM((8, dc), F32), pltpu.VMEM((nh, LANES), F32)],
        compiler_params=pltpu.CompilerParams(
            dimension_semantics=("arbitrary", "arbitrary"), vmem_limit_bytes=VMEM_LIMIT),
        name="inproj_prompt",
    )(x, g, w, wft, bfc, cw, qg, kg, og, seg, eye)


def _attn_prompt_kernel(q_ref, k_ref, v_ref, og_ref, seg_ref, o_ref, m_ref, acc_ref, *, tk):
    qi = pl.program_id(2)
    nhs, tq = q_ref.shape[1:3]
    rep = tk // LANES
    m_ref[...] = jnp.full_like(m_ref, -jnp.inf)
    acc_ref[...] = jnp.zeros_like(acc_ref)

    def tile(ki, masked):
        k0 = pl.multiple_of(ki * tk, tk)
        if masked:
            keep = (lax.broadcasted_iota(jnp.int32, (tq, tk), 1)
                    <= lax.broadcasted_iota(jnp.int32, (tq, tk), 0))
        for hh in range(nhs):
            s = _dotT(q_ref[0, hh], k_ref[0, hh, pl.ds(k0, tk), :])
            if masked:
                s = jnp.where(keep, s, -jnp.inf)
            m_old = m_ref[hh]
            m_new = jnp.maximum(m_old, jnp.max(s, axis=1, keepdims=True))
            p = jnp.exp2((s - jnp.tile(m_new, (1, rep))).astype(BF16))
            alpha = jnp.exp2(m_old - m_new)
            acc_ref[hh] = alpha * acc_ref[hh] + jnp.dot(
                p, v_ref[0, hh, pl.ds(k0, tk), :], preferred_element_type=F32)
            m_ref[hh] = m_new

    def body(ki, carry):
        tile(ki, False)
        return carry

    lax.fori_loop(0, qi, body, 0)
    tile(qi, True)
    low = lax.broadcasted_iota(jnp.int32, (1, LANES), 1) < HEAD_DIM
    seg = seg_ref[...]
    for hp in range(nhs // 2):
        outs = []
        for hh in (2 * hp, 2 * hp + 1):
            a = acc_ref[hh]
            outs.append(a / a[:, HEAD_DIM:HEAD_DIM + 1])
        o = jnp.where(low, outs[0], pltpu.roll(outs[1], HEAD_DIM, 1))
        ms = jnp.dot((o * o).astype(BF16), seg, preferred_element_type=F32)
        o_ref[0, :, hp * LANES:(hp + 1) * LANES] = (
            o * lax.rsqrt(ms + EPS) * og_ref[:, hp * LANES:(hp + 1) * LANES]).astype(BF16)


def _attn_prompt(qa, ka, va, og, seg, *, tq, nhs):
    b, nh, s, _ = qa.shape
    grid = (b, nh // nhs, s // tq)
    wo = nhs * HEAD_DIM
    in_specs = [
        pl.BlockSpec((1, nhs, tq, LANES), lambda i, h, j: (i, h, j, 0)),
        pl.BlockSpec((1, nhs, s, LANES), lambda i, h, j: (i, h, 0, 0)),
        pl.BlockSpec((1, nhs, s, LANES), lambda i, h, j: (i, h, 0, 0)),
        pl.BlockSpec((1, wo), lambda i, h, j: (0, h)),
        _const_spec(seg.shape),
    ]
    return pl.pallas_call(
        functools.partial(_attn_prompt_kernel, tk=tq),
        grid=grid, in_specs=in_specs,
        out_specs=pl.BlockSpec((1, tq, wo), lambda i, h, j: (i, j, h)),
        out_shape=jax.ShapeDtypeStruct((b, s, nh * HEAD_DIM), BF16),
        scratch_shapes=[pltpu.VMEM((nhs, tq, LANES), F32), pltpu.VMEM((nhs, tq, LANES), F32)],
        compiler_params=pltpu.CompilerParams(
            dimension_semantics=("arbitrary", "arbitrary", "arbitrary"), vmem_limit_bytes=VMEM_LIMIT),
        name="attn_prompt",
    )(qa, ka, va, og, seg)


def _mix_ffn_kernel(x_ref, yc_ref, ya_ref, wo_ref, fg_ref, wg_ref, wu_ref, wd_ref, o_ref, *, ff_chunk):
    dc = yc_ref.shape[1]
    mix = jnp.dot(yc_ref[...], wo_ref[0:dc, :], preferred_element_type=F32)
    mix = mix + jnp.dot(ya_ref[...], wo_ref[dc:, :], preferred_element_type=F32)
    h = x_ref[...] + mix
    hn = (h * _rms_scale(h) * fg_ref[...]).astype(BF16)
    dff = wg_ref.shape[1]
    acc = None
    for c0 in range(0, dff, ff_chunk):
        gate = jnp.dot(hn, wg_ref[:, c0:c0 + ff_chunk], preferred_element_type=F32)
        up = jnp.dot(hn, wu_ref[:, c0:c0 + ff_chunk], preferred_element_type=F32)
        a = (gate * jax.nn.sigmoid(gate) * up).astype(BF16)
        part = jnp.dot(a, wd_ref[c0:c0 + ff_chunk, :], preferred_element_type=F32)
        acc = part if acc is None else acc + part
    o_ref[...] = h + acc


def _mix_ffn(x, yc, ya, wo, fg, wg, wu, wd, *, tm, ff_chunk):
    r, d = x.shape
    row_blk = lambda n: pl.BlockSpec((tm, n), lambda i: (i, 0))
    in_specs = [row_blk(d), row_blk(yc.shape[1]), row_blk(ya.shape[1]),
                _const_spec(wo.shape), _const_spec(fg.shape), _const_spec(wg.shape),
                _const_spec(wu.shape), _const_spec(wd.shape)]
    return pl.pallas_call(
        functools.partial(_mix_ffn_kernel, ff_chunk=ff_chunk),
        grid=(r // tm,), in_specs=in_specs, out_specs=row_blk(d),
        out_shape=jax.ShapeDtypeStruct((r, d), F32),
        compiler_params=pltpu.CompilerParams(
            dimension_semantics=("arbitrary",), vmem_limit_bytes=VMEM_LIMIT),
        name="mix_ffn",
    )(x, yc, ya, wo, fg, wg, wu, wd)


def _inproj_sample_kernel(x_ref, g_ref, w_ref, wft_ref, bf_ref, cw_ref, qg_ref, kg_ref, og_ref, seg_ref,
                          st_ref, ycn_ref, q_ref, k_ref, v_ref, lft_ref, gt_ref, cs_ref, *, dc, da, nt):
    nb = x_ref.shape[0] // nt
    x = x_ref[...]
    xn = (x * _rms_scale(x) * g_ref[...]).astype(BF16)
    seg = seg_ref[...]

    def proj(c0, n):
        return jnp.dot(xn, w_ref[:, c0:c0 + n], preferred_element_type=F32)

    ub = proj(0, dc)
    u = proj(dc, dc) * proj(2 * dc, dc)
    ucat = [st_ref[i] for i in range(CONV_WIDTH - 1)] + [u[t * nb:(t + 1) * nb] for t in range(nt)]
    conv = jnp.concatenate(
        [sum(cw_ref[jj:jj + 1] * ucat[t + jj] for jj in range(CONV_WIDTH)) for t in range(nt)], axis=0)
    ycn_ref[...] = _head_norm(ub * conv, og_ref[...], seg).astype(BF16)
    for i in range(CONV_WIDTH - 1):
        cs_ref[i] = ucat[nt + i]

    q_ref[...] = _head_norm(proj(3 * dc, da), qg_ref[...], seg) * (HEAD_DIM ** -0.5)
    k_ref[...] = _head_norm(proj(3 * dc + da, da), kg_ref[...], seg)
    v_ref[...] = proj(3 * dc + 2 * da, da)

    nh = lft_ref.shape[0]
    lf = _log_sigmoid(_dotT(wft_ref[...], xn)[:nh] + bf_ref[...])
    lft_ref[...] = lf
    run = lf[:, 0:nb]
    gs = [run]
    for t in range(1, nt):
        run = run + lf[:, t * nb:(t + 1) * nb]
        gs.append(run)
    gt_ref[...] = jnp.concatenate(gs, axis=1)


def _inproj_sample(x, g, w, wft, bfc, cw, qg, kg, og, seg, st, *, nt):
    r, d = x.shape
    dc = cw.shape[1]
    da = qg.shape[1]
    nh = bfc.shape[0]
    args = (x, g, w, wft, bfc, cw, qg, kg, og, seg, st)
    out_shape = (
        jax.ShapeDtypeStruct((r, dc), BF16),
        jax.ShapeDtypeStruct((r, da), F32),
        jax.ShapeDtypeStruct((r, da), F32),
        jax.ShapeDtypeStruct((r, da), F32),
        jax.ShapeDtypeStruct((nh, r), F32),
        jax.ShapeDtypeStruct((nh, r), F32),
        jax.ShapeDtypeStruct(st.shape, F32),
    )
    full = lambda a: pl.BlockSpec(a.shape, lambda i: (0,) * len(a.shape))
    return pl.pallas_call(
        functools.partial(_inproj_sample_kernel, dc=dc, da=da, nt=nt),
        grid=(1,), in_specs=[full(a) for a in args], out_specs=tuple(full(o) for o in out_shape),
        out_shape=out_shape,
        compiler_params=pltpu.CompilerParams(
            dimension_semantics=("arbitrary",), vmem_limit_bytes=VMEM_LIMIT),
        name="inproj_sample",
    )(*args)


def _ffn_decode_kernel(pt_ref, x_ref, yc_ref, ya_ref, wo_ref, fg_ref, wg_ref, wu_ref, wd_ref,
                       qbd_ref, kn_ref, vn_ref, gneg_ref, og_ref, seg_ref, ck_hbm, cv_hbm, clf_hbm,
                       o_ref, oa_ref, kbuf, vbuf, lbuf, ksem, vsem, lsem, *, ch, ring, nt, nh, ff_chunk):
    i = pl.program_id(0)
    nb, npages = pt_ref.shape
    nch = npages // ch

    def page_copies(b, c, values):
        out = []
        for j in range(ch):
            page = pt_ref[b, npages - 1 - (c * ch + j)]
            s = (c % ring) * ch + j
            if values:
                out.append(pltpu.make_async_copy(cv_hbm.at[page], vbuf.at[s], vsem.at[s]))
            else:
                out.append(pltpu.make_async_copy(ck_hbm.at[page], kbuf.at[s], ksem.at[s]))
                out.append(pltpu.make_async_copy(clf_hbm.at[page], lbuf.at[s], lsem.at[s]))
        return out

    def start_chunk(b, c, values):
        for cp in page_copies(b, c, values):
            cp.start()

    def wait_chunk(b, c, values):
        for cp in page_copies(b, c, values):
            cp.wait()

    @pl.when(i == 0)
    def _():
        for c in range(min(ring, nch)):
            start_chunk(0, c, False)
            start_chunk(0, c, True)

    qbd = qbd_ref[0]
    sn = _dotT(qbd, kn_ref[0]) + gneg_ref[0]
    r_i = lax.broadcasted_iota(jnp.int32, sn.shape, 0)
    c_i = lax.broadcasted_iota(jnp.int32, sn.shape, 1)
    sn = jnp.where(c_i * nh <= r_i, sn, -jnp.inf)
    m = jnp.max(sn, axis=1, keepdims=True)
    p0 = jnp.exp(sn - m)
    l = jnp.sum(p0, axis=1, keepdims=True)
    acc = jnp.dot(p0.astype(BF16), vn_ref[0], preferred_element_type=F32)
    run = jnp.zeros((nh, 1), F32)
    lane = lax.broadcasted_iota(jnp.int32, (nh, PAGE_SIZE), 1)

    def decode_scores(c, run):
        slots = [(c % ring) * ch + j for j in range(ch)]
        decays = []
        for s_ in slots:
            lf = lbuf[s_]
            suf = lf
            sh = 1
            while sh < PAGE_SIZE:
                suf = suf + jnp.where(lane < PAGE_SIZE - sh, pltpu.roll(suf, PAGE_SIZE - sh, 1), 0.0)
                sh *= 2
            decays.append(jnp.concatenate([run + (suf - lf)] * nt, axis=0))
            run = run + suf[:, 0:1]
        kt = jnp.concatenate([kbuf[s_].astype(BF16) for s_ in slots], axis=1)
        return jnp.dot(qbd, kt, preferred_element_type=F32) + jnp.concatenate(decays, axis=1), run

    def decode_values(c, s, m, l, acc):
        slots = [(c % ring) * ch + j for j in range(ch)]
        m_new = jnp.maximum(m, jnp.max(s, axis=1, keepdims=True))
        alpha = jnp.exp(m - m_new)
        p = jnp.exp(s - m_new)
        l = alpha * l + jnp.sum(p, axis=1, keepdims=True)
        vt = jnp.concatenate([vbuf[s_].astype(BF16) for s_ in slots], axis=1)
        return m_new, l, alpha * acc + _dotT(p.astype(BF16), vt)

    def decode_finish(l, acc):
        o = acc / l
        r_o = lax.broadcasted_iota(jnp.int32, o.shape, 0)
        c_o = lax.broadcasted_iota(jnp.int32, o.shape, 1)
        o = jnp.where(c_o // HEAD_DIM == r_o % nh, o, 0.0)
        y = jnp.concatenate([jnp.sum(o[t * nh:(t + 1) * nh], axis=0, keepdims=True) for t in range(nt)], axis=0)
        oa_ref[0] = _head_norm(y, og_ref[...], seg_ref[...])

    dc = yc_ref.shape[1]
    dff = wg_ref.shape[1]
    ffn = {}

    def ffn_first(ph):
        if ph == 0:
            mix = jnp.dot(yc_ref[...], wo_ref[0:dc, :], preferred_element_type=F32)
            mix = mix + jnp.dot(ya_ref[...], wo_ref[dc:, :], preferred_element_type=F32)
            ffn["h"] = x_ref[...] + mix
            return
        c0 = (ph - 1) * ff_chunk
        hn = ffn["hn"]
        ffn["gate"] = jnp.dot(hn, wg_ref[:, c0:c0 + ff_chunk], preferred_element_type=F32)
        ffn["up"] = jnp.dot(hn, wu_ref[:, c0:c0 + ff_chunk], preferred_element_type=F32)

    def ffn_second(ph):
        if ph == 0:
            h = ffn["h"]
            ffn["hn"] = (h * _rms_scale(h) * fg_ref[...]).astype(BF16)
            ffn["acc"] = None
            return
        c0 = (ph - 1) * ff_chunk
        gate = ffn["gate"]
        a = (gate * jax.nn.sigmoid(gate) * ffn["up"]).astype(BF16)
        part = jnp.dot(a, wd_ref[c0:c0 + ff_chunk, :], preferred_element_type=F32)
        ffn["acc"] = part if ffn["acc"] is None else ffn["acc"] + part

    n_ffn = 1 + dff // ff_chunk
    s_prev = None
    for ph in range(max(n_ffn, nch + 1)):
        if ph < nch:
            wait_chunk(i, ph, False)
        if 1 <= ph <= nch:
            wait_chunk(i, ph - 1, True)
        if ph < n_ffn:
            ffn_first(ph)
        if 1 <= ph <= nch:
            m, l, acc = decode_values(ph - 1, s_prev, m, l, acc)
        if ph == nch:
            @pl.when(i + 1 < nb)
            def _():
                for c in range(min(ring, nch)):
                    start_chunk(i + 1, c, False)
                    start_chunk(i + 1, c, True)
            decode_finish(l, acc)
        if ph < n_ffn:
            ffn_second(ph)
        if ph < nch:
            s_prev, run = decode_scores(ph, run)
        if ph + ring < nch:
            start_chunk(i, ph + ring, False)
        if ph >= 1 and ph - 1 + ring < nch:
            start_chunk(i, ph - 1 + ring, True)
    o_ref[...] = ffn["h"] + ffn["acc"]


def _mix_ffn_decode(page_table, x, yc, ya, wo, fg, wg, wu, wd, qbd, kn, vn, gneg, og, seg, ck, cv, clf,
                    *, ch, ring, nt, nh, ff_chunk):
    r, d = x.shape
    nb, npages = page_table.shape
    da = qbd.shape[2]
    assert r % nb == 0 and npages % ch == 0 and wg.shape[1] % ff_chunk == 0
    tm = r // nb
    row_blk = lambda n: pl.BlockSpec((tm, n), lambda i, pt: (i, 0))
    const = lambda a: pl.BlockSpec(a.shape, lambda i, pt: (0,) * a.ndim, pipeline_mode=pl.Buffered(1))
    per_b = lambda a: pl.BlockSpec((1,) + a.shape[1:], lambda i, pt: (i,) + (0,) * (a.ndim - 1))
    hbm = pl.BlockSpec(memory_space=pl.ANY)
    in_specs = [row_blk(d), row_blk(yc.shape[1]), row_blk(ya.shape[1]),
                const(wo), const(fg), const(wg), const(wu), const(wd),
                per_b(qbd), per_b(kn), per_b(vn), per_b(gneg), const(og), const(seg), hbm, hbm, hbm]
    nslot = ring * ch
    grid_spec = pltpu.PrefetchScalarGridSpec(
        num_scalar_prefetch=1, grid=(nb,), in_specs=in_specs,
        out_specs=(row_blk(d), pl.BlockSpec((1, nt, da), lambda i, pt: (i, 0, 0))),
        scratch_shapes=[pltpu.VMEM((nslot, da, PAGE_SIZE), F32), pltpu.VMEM((nslot, da, PAGE_SIZE), F32),
                        pltpu.VMEM((nslot, nh, PAGE_SIZE), F32),
                        pltpu.SemaphoreType.DMA((nslot,)), pltpu.SemaphoreType.DMA((nslot,)),
                        pltpu.SemaphoreType.DMA((nslot,))])
    return pl.pallas_call(
        functools.partial(_ffn_decode_kernel, ch=ch, ring=ring, nt=nt, nh=nh, ff_chunk=ff_chunk),
        grid_spec=grid_spec,
        out_shape=(jax.ShapeDtypeStruct((r, d), F32), jax.ShapeDtypeStruct((nb, nt, da), F32)),
        compiler_params=pltpu.CompilerParams(
            dimension_semantics=("arbitrary",), vmem_limit_bytes=VMEM_LIMIT),
        name="mix_ffn_decode",
    )(page_table, x, yc, ya, wo, fg, wg, wu, wd, qbd, kn, vn, gneg, og, seg, ck, cv, clf)


def _pick(n, pref):
    t = min(n, pref)
    while n % t:
        t //= 2
    return t


def _layer(xp, xs, cache_k, cache_v, cache_logf, state_conv, page_table,
           norm_g, w_in, b_f, conv_w, q_g, k_g, out_g, w_out, ffn_g, w_gate, w_up, w_down):
    bsz, seq, d = xp.shape
    nb, nt, _ = xs.shape
    dc = conv_w.shape[1]
    nh = b_f.shape[0]
    da = nh * HEAD_DIM
    npages = page_table.shape[1]

    g = norm_g.reshape(1, d)
    w_main = w_in[:, :3 * dc + 3 * da].astype(BF16)
    wft = jnp.zeros((16, d), BF16).at[:nh].set(w_in[:, 3 * dc + 3 * da:].T.astype(BF16))
    bfc = b_f.reshape(nh, 1)
    qg = jnp.tile(q_g, nh).reshape(1, da)
    kg = jnp.tile(k_g, nh).reshape(1, da)
    og_c = out_g[:dc].reshape(1, dc)
    og_a = out_g[dc:].reshape(1, da)
    gi = jnp.arange(MXU_DIM) // HEAD_DIM
    seg = jnp.where(gi[:, None] == gi[None, :], 1.0 / HEAD_DIM, 0.0).astype(BF16)
    wo = w_out.astype(BF16)
    fg = ffn_g.reshape(1, d)
    wg = w_gate.astype(BF16)
    wu = w_up.astype(BF16)
    wd = w_down.astype(BF16)
    dff = wg.shape[1]

    tm = _pick(seq, 512)
    eye = jnp.eye(tm, dtype=BF16)
    ycn, qa, ka, va, k_out, v_out, lft, cs_p = _inproj_prompt(
        xp, g, w_main, wft, bfc, conv_w, qg, kg, og_c, seg, eye, tm=tm)
    yan = _attn_prompt(qa, ka, va, og_a, seg[:LANES, :LANES], tq=_pick(seq, 512), nhs=ATTN_HEADS_PER_STEP)
    rows = bsz * seq
    nk_p = k_out.reshape(bsz, seq, nh, HEAD_DIM)
    nv_p = v_out.reshape(bsz, seq, nh, HEAD_DIM)
    nl_p = jnp.transpose(lft, (0, 2, 1))

    rs = nb * nt
    xs_tm = jnp.transpose(xs, (1, 0, 2)).reshape(rs, d)
    st = jnp.transpose(state_conv, (1, 0, 2))
    ycn_s, q_s, k_s, v_s, lft_s, gt_s, cs_s = _inproj_sample(
        xs_tm, g, w_main, wft, bfc, conv_w, qg, kg, og_c, seg, st, nt=nt)

    to_bt = lambda a: jnp.transpose(a.reshape(nt, nb, -1), (1, 0, 2))
    eye_h = jnp.eye(nh, dtype=F32)
    qbd = jnp.einsum('tbhd,hg->bthgd', q_s.reshape(nt, nb, nh, HEAD_DIM), eye_h)
    qbd = qbd.reshape(nb, nt * nh, da).astype(BF16)
    pad_rows = 16 - nt
    kn = jnp.pad(to_bt(k_s), ((0, 0), (0, pad_rows), (0, 0))).astype(BF16)
    vn = jnp.pad(to_bt(v_s), ((0, 0), (0, pad_rows), (0, 0))).astype(BF16)
    gneg = -jnp.transpose(gt_s.reshape(nh, nt, nb), (2, 0, 1))
    gneg = jnp.pad(gneg, ((0, 0), (0, 0), (0, 16 - nt)))
    gneg = jnp.tile(gneg, (1, nt, 1))
    nphys = cache_k.shape[0]
    ck = jnp.transpose(cache_k, (0, 2, 3, 1)).reshape(nphys, da, PAGE_SIZE)
    cv = jnp.transpose(cache_v, (0, 2, 3, 1)).reshape(nphys, da, PAGE_SIZE)
    clf = jnp.transpose(cache_logf, (0, 2, 1))
    yp, yan_s = _mix_ffn_decode(
        page_table, xp.reshape(rows, d), ycn.reshape(rows, dc), yan.reshape(rows, da), wo, fg, wg, wu, wd,
        qbd, kn, vn, gneg, og_a, seg, ck, cv, clf,
        ch=_pick(npages, DECODE_PAGES_PER_CHUNK), ring=DECODE_RING_CHUNKS, nt=nt, nh=nh, ff_chunk=MXU_DIM)
    y_prompt = yp.reshape(bsz, seq, d)
    ys = _mix_ffn(xs.reshape(rs, d), to_bt(ycn_s).reshape(rs, dc), yan_s.reshape(rs, da).astype(BF16),
                  wo, fg, wg, wu, wd, tm=_pick(rs, 512), ff_chunk=dff)
    y_sample = ys.reshape(nb, nt, d)
    nk_s = to_bt(k_s).reshape(nb, nt, nh, HEAD_DIM)
    nv_s = to_bt(v_s).reshape(nb, nt, nh, HEAD_DIM)
    nl_s = jnp.transpose(lft_s.reshape(nh, nt, nb), (2, 1, 0))
    cs_s = jnp.transpose(cs_s, (1, 0, 2))
    return (y_prompt, y_sample, nk_p, nv_p, nl_p, cs_p, nk_s, nv_s, nl_s, cs_s)


def kernel(x_prompt, x_sample, cache_k, cache_v, cache_logf, state_conv, page_table,
           norm_mix_g, w_in, b_f, conv_w, q_norm_g, k_norm_g, out_norm_g, w_out,
           norm_ffn_g, w_gate, w_up, w_down):
    depth = w_in.shape[0]
    xp, xs = x_prompt, x_sample
    outs = []
    for l in range(depth):
        res = _layer(xp, xs, cache_k[l], cache_v[l], cache_logf[l], state_conv[l], page_table,
                     norm_mix_g[l], w_in[l], b_f[l], conv_w[l], q_norm_g[l], k_norm_g[l], out_norm_g[l],
                     w_out[l], norm_ffn_g[l], w_gate[l], w_up[l], w_down[l])
        xp, xs = res[0], res[1]
        outs.append(res[2:])
    stacked = tuple(jnp.stack([o[i] for o in outs]) for i in range(8))
    return (xp, xs) + stacked
```

```python
import functools

import jax
import jax.numpy as jnp
from jax import lax
from jax.experimental import pallas as pl
from jax.experimental.pallas import tpu as pltpu

HEAD_DIM = 64
CONV_WIDTH = 3
PAGE_SIZE = 128
EPS = 1e-6
LOG2E = 1.4426950408889634
LANES = 128
MXU_DIM = 256
VMEM_LIMIT = 56 * 1024 * 1024
ATTN_HEADS_PER_STEP = 8
DECODE_PAGES_PER_CHUNK = 8
DECODE_RING_CHUNKS = 3

F32 = jnp.float32
BF16 = jnp.bfloat16


def _log_sigmoid(x):
    return jnp.minimum(x, 0.0) - jnp.log1p(jnp.exp(-jnp.abs(x)))


def _rms_scale(x):
    return lax.rsqrt(jnp.mean(x * x, axis=-1, keepdims=True) + EPS)


def _head_mean_sq(y, seg):
    y2 = (y * y).astype(BF16)
    parts = [jnp.dot(y2[:, c:c + MXU_DIM], seg, preferred_element_type=F32)
             for c in range(0, y.shape[1], MXU_DIM)]
    return parts[0] if len(parts) == 1 else jnp.concatenate(parts, axis=1)


def _head_norm(y, gain, seg):
    return y * lax.rsqrt(_head_mean_sq(y, seg) + EPS) * gain


def _dotT(a, b):
    return lax.dot_general(a, b, (((1,), (1,)), ((), ())), preferred_element_type=F32)


def _const_spec(shape):
    nd = len(shape)
    return pl.BlockSpec(shape, lambda *_: (0,) * nd, pipeline_mode=pl.Buffered(1))


def _split3(x):
    hi = x.astype(BF16)
    r1 = x - hi.astype(F32)
    mid = r1.astype(BF16)
    lo = (r1 - mid.astype(F32)).astype(BF16)
    return hi, mid, lo


def _store_head_major(y, extra, o_ref):
    low = lax.broadcasted_iota(jnp.int32, (1, LANES), 1) < HEAD_DIM
    for h in range(o_ref.shape[1]):
        pair = y[:, (h // 2) * LANES:(h // 2 + 1) * LANES]
        if h % 2:
            pair = pltpu.roll(pair, HEAD_DIM, 1)
        o_ref[0, h] = jnp.where(low, pair, extra(h)).astype(BF16)


def _inproj_prompt_kernel(x_ref, g_ref, w_ref, wft_ref, bf_ref, cw_ref, qg_ref, kg_ref, og_ref, seg_ref, eye_ref,
                          ycn_ref, qa_ref, ka_ref, va_ref, kout_ref, vout_ref, lft_ref, cs_ref,
                          ucar_ref, ccar_ref, *, dc, da):
    j = pl.program_id(1)
    tm = x_ref.shape[1]
    nh = lft_ref.shape[1]

    @pl.when(j == 0)
    def _():
        ucar_ref[...] = jnp.zeros_like(ucar_ref)
        ccar_ref[...] = jnp.zeros_like(ccar_ref)

    x = x_ref[0]
    xn = (x * _rms_scale(x) * g_ref[...]).astype(BF16)
    seg = seg_ref[...]

    def proj(c0, n):
        return jnp.dot(xn, w_ref[:, c0:c0 + n], preferred_element_type=F32)

    lf = _log_sigmoid(_dotT(wft_ref[...], xn)[:nh] + bf_ref[...])
    lft_ref[0] = lf
    lane_t = lax.broadcasted_iota(jnp.int32, lf.shape, 1)
    c = lf
    sh = 1
    while sh < tm:
        c = c + jnp.where(lane_t >= sh, pltpu.roll(c, sh, 1), 0.0)
        sh *= 2
    c = c + ccar_ref[:, 0:1]
    ccar_ref[...] = jnp.broadcast_to(c[:, tm - 1:tm], ccar_ref.shape)
    parts = _split3(c * (-LOG2E))
    c3 = jnp.concatenate([jnp.zeros((HEAD_DIM, tm), BF16), *parts,
                          jnp.zeros((LANES - HEAD_DIM - 3 * nh, tm), BF16)], axis=0)
    bias_cols = _dotT(eye_ref[...], c3)

    ub = proj(0, dc)
    u = proj(dc, dc) * proj(2 * dc, dc)
    prev = ucar_ref[...]
    row = lax.broadcasted_iota(jnp.int32, (tm, 1), 0)
    u1 = jnp.where(row == 0, prev[7:8], pltpu.roll(u, 1, 0))
    u2 = jnp.where(row == 0, prev[6:7], jnp.where(row == 1, prev[7:8], pltpu.roll(u, 2, 0)))
    yc = ub * (cw_ref[0:1] * u2 + cw_ref[1:2] * u1 + cw_ref[2:3] * u)
    ycn_ref[0] = _head_norm(yc, og_ref[...], seg).astype(BF16)
    ucar_ref[...] = u[tm - 8:tm]
    cs_ref[0] = u[tm - (CONV_WIDTH - 1):tm]

    lane = lax.broadcasted_iota(jnp.int32, (1, LANES), 1)
    q = _head_norm(proj(3 * dc, da), qg_ref[...], seg) * (HEAD_DIM ** -0.5 * LOG2E)
    _store_head_major(
        q, lambda h: jnp.where((lane >= HEAD_DIM) & (lane < HEAD_DIM + 3 * nh) & (lane % nh == h), 1.0, 0.0), qa_ref)
    k = _head_norm(proj(3 * dc + da, da), kg_ref[...], seg)
    kout_ref[0] = k
    _store_head_major(k, lambda h: bias_cols, ka_ref)
    v = proj(3 * dc + 2 * da, da)
    vout_ref[0] = v
    _store_head_major(v, lambda h: jnp.where(lane == HEAD_DIM, 1.0, 0.0), va_ref)


def _inproj_prompt(x, g, w, wft, bfc, cw, qg, kg, og, seg, eye, *, tm):
    b, s, d = x.shape
    dc = cw.shape[1]
    da = qg.shape[1]
    nh = bfc.shape[0]
    grid = (b, s // tm)
    row_blk = lambda n: pl.BlockSpec((1, tm, n), lambda i, j: (i, j, 0))
    head_blk = pl.BlockSpec((1, nh, tm, LANES), lambda i, j: (i, 0, j, 0))
    out_shape = (
        jax.ShapeDtypeStruct((b, s, dc), BF16),
        jax.ShapeDtypeStruct((b, nh, s, LANES), BF16),
        jax.ShapeDtypeStruct((b, nh, s, LANES), BF16),
        jax.ShapeDtypeStruct((b, nh, s, LANES), BF16),
        jax.ShapeDtypeStruct((b, s, da), F32),
        jax.ShapeDtypeStruct((b, s, da), F32),
        jax.ShapeDtypeStruct((b, nh, s), F32),
        jax.ShapeDtypeStruct((b, CONV_WIDTH - 1, dc), F32),
    )
    out_specs = (
        row_blk(dc), head_blk, head_blk, head_blk, row_blk(da), row_blk(da),
        pl.BlockSpec((1, nh, tm), lambda i, j: (i, 0, j)),
        pl.BlockSpec((1, CONV_WIDTH - 1, dc), lambda i, j: (i, 0, 0)),
    )
    in_specs = [
        row_blk(d), _const_spec(g.shape), _const_spec(w.shape), _const_spec(wft.shape),
        _const_spec(bfc.shape), _const_spec(cw.shape), _const_spec(qg.shape), _const_spec(kg.shape),
        _const_spec(og.shape), _const_spec(seg.shape), _const_spec(eye.shape),
    ]
    return pl.pallas_call(
        functools.partial(_inproj_prompt_kernel, dc=dc, da=da),
        grid=grid, in_specs=in_specs, out_specs=out_specs, out_shape=out_shape,
        scratch_shapes=[pltpu.VMEM((8, dc), F32), pltpu.VMEM((nh, LANES), F32)],
        compiler_params=pltpu.CompilerParams(
            dimension_semantics=("arbitrary", "arbitrary"), vmem_limit_bytes=VMEM_LIMIT),
        name="inproj_prompt",
    )(x, g, w, wft, bfc, cw, qg, kg, og, seg, eye)


def _attn_prompt_kernel(q_ref, k_ref, v_ref, og_ref, seg_ref, o_ref, m_ref, acc_ref, *, tk):
    qi = pl.program_id(2)
    nhs, tq = q_ref.shape[1:3]
    rep = tk // LANES
    m_ref[...] = jnp.full_like(m_ref, -jnp.inf)
    acc_ref[...] = jnp.zeros_like(acc_ref)

    def tile(ki, r0, nr, masked):
        k0 = pl.multiple_of(ki * tk, tk)
        if masked:
            keep = (lax.broadcasted_iota(jnp.int32, (nr, tk), 1)
                    <= lax.broadcasted_iota(jnp.int32, (nr, tk), 0))
        for hh in range(nhs):
            s = _dotT(q_ref[0, hh, r0:r0 + nr, :], k_ref[0, hh, pl.ds(k0, tk), :])
            if masked:
                s = jnp.where(keep, s, -jnp.inf)
            m_old = m_ref[hh, r0:r0 + nr, :]
            m_new = jnp.maximum(m_old, jnp.max(s, axis=1, keepdims=True))
            p = jnp.exp2((s - jnp.tile(m_new, (1, rep))).astype(BF16))
            alpha = jnp.exp2(m_old - m_new)
            acc_ref[hh, r0:r0 + nr, :] = alpha * acc_ref[hh, r0:r0 + nr, :] + jnp.dot(
                p, v_ref[0, hh, pl.ds(k0, tk), :], preferred_element_type=F32)
            m_ref[hh, r0:r0 + nr, :] = m_new

    def body(ki, carry):
        tile(ki, 0, tq, False)
        return carry

    sub = tq // tk
    lax.fori_loop(0, qi * sub, body, 0)
    for j in range(sub):
        tile(qi * sub + j, j * tk, tq - j * tk, True)
    low = lax.broadcasted_iota(jnp.int32, (1, LANES), 1) < HEAD_DIM
    seg = seg_ref[...]
    for hp in range(nhs // 2):
        outs = []
        for hh in (2 * hp, 2 * hp + 1):
            a = acc_ref[hh]
            outs.append(a / a[:, HEAD_DIM:HEAD_DIM + 1])
        o = jnp.where(low, outs[0], pltpu.roll(outs[1], HEAD_DIM, 1))
        ms = jnp.dot((o * o).astype(BF16), seg, preferred_element_type=F32)
        o_ref[0, :, hp * LANES:(hp + 1) * LANES] = (
            o * lax.rsqrt(ms + EPS) * og_ref[:, hp * LANES:(hp + 1) * LANES]).astype(BF16)


def _attn_prompt(qa, ka, va, og, seg, *, tq, tk, nhs):
    b, nh, s, _ = qa.shape
    grid = (b, nh // nhs, s // tq)
    wo = nhs * HEAD_DIM
    in_specs = [
        pl.BlockSpec((1, nhs, tq, LANES), lambda i, h, j: (i, h, j, 0)),
        pl.BlockSpec((1, nhs, s, LANES), lambda i, h, j: (i, h, 0, 0)),
        pl.BlockSpec((1, nhs, s, LANES), lambda i, h, j: (i, h, 0, 0)),
        pl.BlockSpec((1, wo), lambda i, h, j: (0, h)),
        _const_spec(seg.shape),
    ]
    return pl.pallas_call(
        functools.partial(_attn_prompt_kernel, tk=tk),
        grid=grid, in_specs=in_specs,
        out_specs=pl.BlockSpec((1, tq, wo), lambda i, h, j: (i, j, h)),
        out_shape=jax.ShapeDtypeStruct((b, s, nh * HEAD_DIM), BF16),
        scratch_shapes=[pltpu.VMEM((nhs, tq, LANES), F32), pltpu.VMEM((nhs, tq, LANES), F32)],
        compiler_params=pltpu.CompilerParams(
            dimension_semantics=("arbitrary", "arbitrary", "arbitrary"), vmem_limit_bytes=VMEM_LIMIT),
        name="attn_prompt",
    )(qa, ka, va, og, seg)


def _mix_ffn_kernel(x_ref, yc_ref, ya_ref, wo_ref, fg_ref, wg_ref, wu_ref, wd_ref, o_ref, *, ff_chunk):
    dc = yc_ref.shape[1]
    mix = jnp.dot(yc_ref[...], wo_ref[0:dc, :], preferred_element_type=F32)
    mix = mix + jnp.dot(ya_ref[...], wo_ref[dc:, :], preferred_element_type=F32)
    h = x_ref[...] + mix
    hn = (h * _rms_scale(h) * fg_ref[...]).astype(BF16)
    dff = wg_ref.shape[1]
    acc = None
    for c0 in range(0, dff, ff_chunk):
        gate = jnp.dot(hn, wg_ref[:, c0:c0 + ff_chunk], preferred_element_type=F32)
        up = jnp.dot(hn, wu_ref[:, c0:c0 + ff_chunk], preferred_element_type=F32)
        a = (gate * jax.nn.sigmoid(gate) * up).astype(BF16)
        part = jnp.dot(a, wd_ref[c0:c0 + ff_chunk, :], preferred_element_type=F32)
        acc = part if acc is None else acc + part
    o_ref[...] = h + acc


def _mix_ffn(x, yc, ya, wo, fg, wg, wu, wd, *, tm, ff_chunk):
    r, d = x.shape
    row_blk = lambda n: pl.BlockSpec((tm, n), lambda i: (i, 0))
    in_specs = [row_blk(d), row_blk(yc.shape[1]), row_blk(ya.shape[1]),
                _const_spec(wo.shape), _const_spec(fg.shape), _const_spec(wg.shape),
                _const_spec(wu.shape), _const_spec(wd.shape)]
    return pl.pallas_call(
        functools.partial(_mix_ffn_kernel, ff_chunk=ff_chunk),
        grid=(r // tm,), in_specs=in_specs, out_specs=row_blk(d),
        out_shape=jax.ShapeDtypeStruct((r, d), F32),
        compiler_params=pltpu.CompilerParams(
            dimension_semantics=("arbitrary",), vmem_limit_bytes=VMEM_LIMIT),
        name="mix_ffn",
    )(x, yc, ya, wo, fg, wg, wu, wd)


def _inproj_sample_kernel(x_ref, g_ref, w_ref, wft_ref, bf_ref, cw_ref, qg_ref, kg_ref, og_ref, seg_ref,
                          st_ref, ycn_ref, q_ref, k_ref, v_ref, lft_ref, gt_ref, cs_ref, *, dc, da, nt):
    nb = x_ref.shape[0] // nt
    x = x_ref[...]
    xn = (x * _rms_scale(x) * g_ref[...]).astype(BF16)
    seg = seg_ref[...]

    def proj(c0, n):
        return jnp.dot(xn, w_ref[:, c0:c0 + n], preferred_element_type=F32)

    ub = proj(0, dc)
    u = proj(dc, dc) * proj(2 * dc, dc)
    ucat = [st_ref[i] for i in range(CONV_WIDTH - 1)] + [u[t * nb:(t + 1) * nb] for t in range(nt)]
    conv = jnp.concatenate(
        [sum(cw_ref[jj:jj + 1] * ucat[t + jj] for jj in range(CONV_WIDTH)) for t in range(nt)], axis=0)
    ycn_ref[...] = _head_norm(ub * conv, og_ref[...], seg).astype(BF16)
    for i in range(CONV_WIDTH - 1):
        cs_ref[i] = ucat[nt + i]

    q_ref[...] = _head_norm(proj(3 * dc, da), qg_ref[...], seg) * (HEAD_DIM ** -0.5)
    k_ref[...] = _head_norm(proj(3 * dc + da, da), kg_ref[...], seg)
    v_ref[...] = proj(3 * dc + 2 * da, da)

    nh = lft_ref.shape[0]
    lf = _log_sigmoid(_dotT(wft_ref[...], xn)[:nh] + bf_ref[...])
    lft_ref[...] = lf
    run = lf[:, 0:nb]
    gs = [run]
    for t in range(1, nt):
        run = run + lf[:, t * nb:(t + 1) * nb]
        gs.append(run)
    gt_ref[...] = jnp.concatenate(gs, axis=1)


def _inproj_sample(x, g, w, wft, bfc, cw, qg, kg, og, seg, st, *, nt):
    r, d = x.shape
    dc = cw.shape[1]
    da = qg.shape[1]
    nh = bfc.shape[0]
    args = (x, g, w, wft, bfc, cw, qg, kg, og, seg, st)
    out_shape = (
        jax.ShapeDtypeStruct((r, dc), BF16),
        jax.ShapeDtypeStruct((r, da), F32),
        jax.ShapeDtypeStruct((r, da), F32),
        jax.ShapeDtypeStruct((r, da), F32),
        jax.ShapeDtypeStruct((nh, r), F32),
        jax.ShapeDtypeStruct((nh, r), F32),
        jax.ShapeDtypeStruct(st.shape, F32),
    )
    full = lambda a: pl.BlockSpec(a.shape, lambda i: (0,) * len(a.shape))
    return pl.pallas_call(
        functools.partial(_inproj_sample_kernel, dc=dc, da=da, nt=nt),
        grid=(1,), in_specs=[full(a) for a in args], out_specs=tuple(full(o) for o in out_shape),
        out_shape=out_shape,
        compiler_params=pltpu.CompilerParams(
            dimension_semantics=("arbitrary",), vmem_limit_bytes=VMEM_LIMIT),
        name="inproj_sample",
    )(*args)


def _ffn_decode_kernel(pt_ref, x_ref, yc_ref, ya_ref, wo_ref, fg_ref, wg_ref, wu_ref, wd_ref,
                       qbd_ref, kn_ref, vn_ref, gneg_ref, og_ref, seg_ref, ck_hbm, cv_hbm, clf_hbm,
                       o_ref, oa_ref, kbuf, vbuf, lbuf, ksem, vsem, lsem, *, ch, ring, nt, nh, ff_chunk):
    i = pl.program_id(0)
    nb, npages = pt_ref.shape
    nch = npages // ch

    def page_copies(b, c, values):
        out = []
        for j in range(ch):
            page = pt_ref[b, npages - 1 - (c * ch + j)]
            s = (c % ring) * ch + j
            if values:
                out.append(pltpu.make_async_copy(cv_hbm.at[page], vbuf.at[s], vsem.at[s]))
            else:
                out.append(pltpu.make_async_copy(ck_hbm.at[page], kbuf.at[s], ksem.at[s]))
                out.append(pltpu.make_async_copy(clf_hbm.at[page], lbuf.at[s], lsem.at[s]))
        return out

    def start_chunk(b, c, values):
        for cp in page_copies(b, c, values):
            cp.start()

    def wait_chunk(b, c, values):
        for cp in page_copies(b, c, values):
            cp.wait()

    @pl.when(i == 0)
    def _():
        for c in range(min(ring, nch)):
            start_chunk(0, c, False)
            start_chunk(0, c, True)

    qbd = qbd_ref[0]
    sn = _dotT(qbd, kn_ref[0]) + gneg_ref[0]
    r_i = lax.broadcasted_iota(jnp.int32, sn.shape, 0)
    c_i = lax.broadcasted_iota(jnp.int32, sn.shape, 1)
    sn = jnp.where(c_i * nh <= r_i, sn, -jnp.inf)
    m = jnp.max(sn, axis=1, keepdims=True)
    p0 = jnp.exp(sn - m)
    l = jnp.sum(p0, axis=1, keepdims=True)
    acc = jnp.dot(p0.astype(BF16), vn_ref[0], preferred_element_type=F32)
    run = jnp.zeros((nh, 1), F32)
    lane = lax.broadcasted_iota(jnp.int32, (nh, PAGE_SIZE), 1)

    def decode_scores(c, run):
        slots = [(c % ring) * ch + j for j in range(ch)]
        decays = []
        for s_ in slots:
            lf = lbuf[s_]
            suf = lf
            sh = 1
            while sh < PAGE_SIZE:
                suf = suf + jnp.where(lane < PAGE_SIZE - sh, pltpu.roll(suf, PAGE_SIZE - sh, 1), 0.0)
                sh *= 2
            decays.append(jnp.concatenate([run + (suf - lf)] * nt, axis=0))
            run = run + suf[:, 0:1]
        kt = jnp.concatenate([kbuf[s_].astype(BF16) for s_ in slots], axis=1)
        return jnp.dot(qbd, kt, preferred_element_type=F32) + jnp.concatenate(decays, axis=1), run

    def decode_values(c, s, m, l, acc):
        slots = [(c % ring) * ch + j for j in range(ch)]
        m_new = jnp.maximum(m, jnp.max(s, axis=1, keepdims=True))
        alpha = jnp.exp(m - m_new)
        p = jnp.exp(s - m_new)
        l = alpha * l + jnp.sum(p, axis=1, keepdims=True)
        vt = jnp.concatenate([vbuf[s_].astype(BF16) for s_ in slots], axis=1)
        return m_new, l, alpha * acc + _dotT(p.astype(BF16), vt)

    def decode_finish(l, acc):
        o = acc / l
        r_o = lax.broadcasted_iota(jnp.int32, o.shape, 0)
        c_o = lax.broadcasted_iota(jnp.int32, o.shape, 1)
        o = jnp.where(c_o // HEAD_DIM == r_o % nh, o, 0.0)
        y = jnp.concatenate([jnp.sum(o[t * nh:(t + 1) * nh], axis=0, keepdims=True) for t in range(nt)], axis=0)
        oa_ref[0] = _head_norm(y, og_ref[...], seg_ref[...])

    dc = yc_ref.shape[1]
    dff = wg_ref.shape[1]
    ffn = {}

    def ffn_first(ph):
        if ph == 0:
            mix = jnp.dot(yc_ref[...], wo_ref[0:dc, :], preferred_element_type=F32)
            mix = mix + jnp.dot(ya_ref[...], wo_ref[dc:, :], preferred_element_type=F32)
            ffn["h"] = x_ref[...] + mix
            return
        c0 = (ph - 1) * ff_chunk
        hn = ffn["hn"]
        ffn["gate"] = jnp.dot(hn, wg_ref[:, c0:c0 + ff_chunk], preferred_element_type=F32)
        ffn["up"] = jnp.dot(hn, wu_ref[:, c0:c0 + ff_chunk], preferred_element_type=F32)

    def ffn_second(ph):
        if ph == 0:
            h = ffn["h"]
            ffn["hn"] = (h * _rms_scale(h) * fg_ref[...]).astype(BF16)
            ffn["acc"] = None
            return
        c0 = (ph - 1) * ff_chunk
        gate = ffn["gate"]
        a = (gate * jax.nn.sigmoid(gate) * ffn["up"]).astype(BF16)
        part = jnp.dot(a, wd_ref[c0:c0 + ff_chunk, :], preferred_element_type=F32)
        ffn["acc"] = part if ffn["acc"] is None else ffn["acc"] + part

    n_ffn = 1 + dff // ff_chunk
    s_prev = None
    for ph in range(max(n_ffn, nch + 1)):
        if ph < nch:
            wait_chunk(i, ph, False)
        if 1 <= ph <= nch:
            wait_chunk(i, ph - 1, True)
        if ph < n_ffn:
            ffn_first(ph)
        if 1 <= ph <= nch:
            m, l, acc = decode_values(ph - 1, s_prev, m, l, acc)
        if ph == nch:
            @pl.when(i + 1 < nb)
            def _():
                for c in range(min(ring, nch)):
                    start_chunk(i + 1, c, False)
                    start_chunk(i + 1, c, True)
            decode_finish(l, acc)
        if ph < n_ffn:
            ffn_second(ph)
        if ph < nch:
            s_prev, run = decode_scores(ph, run)
        if ph + ring < nch:
            start_chunk(i, ph + ring, False)
        if ph >= 1 and ph - 1 + ring < nch:
            start_chunk(i, ph - 1 + ring, True)
    o_ref[...] = ffn["h"] + ffn["acc"]


def _mix_ffn_decode(page_table, x, yc, ya, wo, fg, wg, wu, wd, qbd, kn, vn, gneg, og, seg, ck, cv, clf,
                    *, ch, ring, nt, nh, ff_chunk):
    r, d = x.shape
    nb, npages = page_table.shape
    da = qbd.shape[2]
    assert r % nb == 0 and npages % ch == 0 and wg.shape[1] % ff_chunk == 0
    tm = r // nb
    row_blk = lambda n: pl.BlockSpec((tm, n), lambda i, pt: (i, 0))
    const = lambda a: pl.BlockSpec(a.shape, lambda i, pt: (0,) * a.ndim, pipeline_mode=pl.Buffered(1))
    per_b = lambda a: pl.BlockSpec((1,) + a.shape[1:], lambda i, pt: (i,) + (0,) * (a.ndim - 1))
    hbm = pl.BlockSpec(memory_space=pl.ANY)
    in_specs = [row_blk(d), row_blk(yc.shape[1]), row_blk(ya.shape[1]),
                const(wo), const(fg), const(wg), const(wu), const(wd),
                per_b(qbd), per_b(kn), per_b(vn), per_b(gneg), const(og), const(seg), hbm, hbm, hbm]
    nslot = ring * ch
    grid_spec = pltpu.PrefetchScalarGridSpec(
        num_scalar_prefetch=1, grid=(nb,), in_specs=in_specs,
        out_specs=(row_blk(d), pl.BlockSpec((1, nt, da), lambda i, pt: (i, 0, 0))),
        scratch_shapes=[pltpu.VMEM((nslot, da, PAGE_SIZE), F32), pltpu.VMEM((nslot, da, PAGE_SIZE), F32),
                        pltpu.VMEM((nslot, nh, PAGE_SIZE), F32),
                        pltpu.SemaphoreType.DMA((nslot,)), pltpu.SemaphoreType.DMA((nslot,)),
                        pltpu.SemaphoreType.DMA((nslot,))])
    return pl.pallas_call(
        functools.partial(_ffn_decode_kernel, ch=ch, ring=ring, nt=nt, nh=nh, ff_chunk=ff_chunk),
        grid_spec=grid_spec,
        out_shape=(jax.ShapeDtypeStruct((r, d), F32), jax.ShapeDtypeStruct((nb, nt, da), F32)),
        compiler_params=pltpu.CompilerParams(
            dimension_semantics=("arbitrary",), vmem_limit_bytes=VMEM_LIMIT),
        name="mix_ffn_decode",
    )(page_table, x, yc, ya, wo, fg, wg, wu, wd, qbd, kn, vn, gneg, og, seg, ck, cv, clf)


def _pick(n, pref):
    t = min(n, pref)
    while n % t:
        t //= 2
    return t


def _layer(xp, xs, cache_k, cache_v, cache_logf, state_conv, page_table,
           norm_g, w_in, b_f, conv_w, q_g, k_g, out_g, w_out, ffn_g, w_gate, w_up, w_down):
    bsz, seq, d = xp.shape
    nb, nt, _ = xs.shape
    dc = conv_w.shape[1]
    nh = b_f.shape[0]
    da = nh * HEAD_DIM
    npages = page_table.shape[1]

    g = norm_g.reshape(1, d)
    w_main = w_in[:, :3 * dc + 3 * da].astype(BF16)
    wft = jnp.zeros((16, d), BF16).at[:nh].set(w_in[:, 3 * dc + 3 * da:].T.astype(BF16))
    bfc = b_f.reshape(nh, 1)
    qg = jnp.tile(q_g, nh).reshape(1, da)
    kg = jnp.tile(k_g, nh).reshape(1, da)
    og_c = out_g[:dc].reshape(1, dc)
    og_a = out_g[dc:].reshape(1, da)
    gi = jnp.arange(MXU_DIM) // HEAD_DIM
    seg = jnp.where(gi[:, None] == gi[None, :], 1.0 / HEAD_DIM, 0.0).astype(BF16)
    wo = w_out.astype(BF16)
    fg = ffn_g.reshape(1, d)
    wg = w_gate.astype(BF16)
    wu = w_up.astype(BF16)
    wd = w_down.astype(BF16)
    dff = wg.shape[1]

    tm = _pick(seq, 512)
    eye = jnp.eye(tm, dtype=BF16)
    ycn, qa, ka, va, k_out, v_out, lft, cs_p = _inproj_prompt(
        xp, g, w_main, wft, bfc, conv_w, qg, kg, og_c, seg, eye, tm=tm)
    yan = _attn_prompt(qa, ka, va, og_a, seg[:LANES, :LANES], tq=_pick(seq, 1024), tk=_pick(seq, 512), nhs=ATTN_HEADS_PER_STEP)
    rows = bsz * seq
    nk_p = k_out.reshape(bsz, seq, nh, HEAD_DIM)
    nv_p = v_out.reshape(bsz, seq, nh, HEAD_DIM)
    nl_p = jnp.transpose(lft, (0, 2, 1))

    rs = nb * nt
    xs_tm = jnp.transpose(xs, (1, 0, 2)).reshape(rs, d)
    st = jnp.transpose(state_conv, (1, 0, 2))
    ycn_s, q_s, k_s, v_s, lft_s, gt_s, cs_s = _inproj_sample(
        xs_tm, g, w_main, wft, bfc, conv_w, qg, kg, og_c, seg, st, nt=nt)

    to_bt = lambda a: jnp.transpose(a.reshape(nt, nb, -1), (1, 0, 2))
    eye_h = jnp.eye(nh, dtype=F32)
    qbd = jnp.einsum('tbhd,hg->bthgd', q_s.reshape(nt, nb, nh, HEAD_DIM), eye_h)
    qbd = qbd.reshape(nb, nt * nh, da).astype(BF16)
    pad_rows = 16 - nt
    kn = jnp.pad(to_bt(k_s), ((0, 0), (0, pad_rows), (0, 0))).astype(BF16)
    vn = jnp.pad(to_bt(v_s), ((0, 0), (0, pad_rows), (0, 0))).astype(BF16)
    gneg = -jnp.transpose(gt_s.reshape(nh, nt, nb), (2, 0, 1))
    gneg = jnp.pad(gneg, ((0, 0), (0, 0), (0, 16 - nt)))
    gneg = jnp.tile(gneg, (1, nt, 1))
    nphys = cache_k.shape[0]
    ck = jnp.transpose(cache_k, (0, 2, 3, 1)).reshape(nphys, da, PAGE_SIZE)
    cv = jnp.transpose(cache_v, (0, 2, 3, 1)).reshape(nphys, da, PAGE_SIZE)
    clf = jnp.transpose(cache_logf, (0, 2, 1))
    yp, yan_s = _mix_ffn_decode(
        page_table, xp.reshape(rows, d), ycn.reshape(rows, dc), yan.reshape(rows, da), wo, fg, wg, wu, wd,
        qbd, kn, vn, gneg, og_a, seg, ck, cv, clf,
        ch=_pick(npages, DECODE_PAGES_PER_CHUNK), ring=DECODE_RING_CHUNKS, nt=nt, nh=nh, ff_chunk=MXU_DIM)
    y_prompt = yp.reshape(bsz, seq, d)
    ys = _mix_ffn(xs.reshape(rs, d), to_bt(ycn_s).reshape(rs, dc), yan_s.reshape(rs, da).astype(BF16),
                  wo, fg, wg, wu, wd, tm=_pick(rs, 512), ff_chunk=dff)
    y_sample = ys.reshape(nb, nt, d)
    nk_s = to_bt(k_s).reshape(nb, nt, nh, HEAD_DIM)
    nv_s = to_bt(v_s).reshape(nb, nt, nh, HEAD_DIM)
    nl_s = jnp.transpose(lft_s.reshape(nh, nt, nb), (2, 1, 0))
    cs_s = jnp.transpose(cs_s, (1, 0, 2))
    return (y_prompt, y_sample, nk_p, nv_p, nl_p, cs_p, nk_s, nv_s, nl_s, cs_s)


def kernel(x_prompt, x_sample, cache_k, cache_v, cache_logf, state_conv, page_table,
           norm_mix_g, w_in, b_f, conv_w, q_norm_g, k_norm_g, out_norm_g, w_out,
           norm_ffn_g, w_gate, w_up, w_down):
    depth = w_in.shape[0]
    xp, xs = x_prompt, x_sample
    outs = []
    for l in range(depth):
        res = _layer(xp, xs, cache_k[l], cache_v[l], cache_logf[l], state_conv[l], page_table,
                     norm_mix_g[l], w_in[l], b_f[l], conv_w[l], q_norm_g[l], k_norm_g[l], out_norm_g[l],
                     w_out[l], norm_ffn_g[l], w_gate[l], w_up[l], w_down[l])
        xp, xs = res[0], res[1]
        outs.append(res[2:])
    stacked = tuple(jnp.stack([o[i] for o in outs]) for i in range(8))
    return (xp, xs) + stacked
```
